```python
import jax, jax.numpy as jnp
from jax import lax
import numpy as np

D_MODEL = 1024
BATCH = 4
SEQ = 8192
DEPTH = 2

CHUNK = 64
D_MIX = D_MODEL
D_LRU = D_MIX // 2
D_ATTN = D_MIX - D_LRU
HEAD_DIM = 64
N_ATTN_HEADS = D_ATTN // HEAD_DIM
LRU_BLOCKS = 8
LRU_BLOCK_W = D_LRU // LRU_BLOCKS
CONV_W = 4
LRU_C = 8.0
Q_BLOCK = 128
N_IN = 2 * D_LRU + 4 * D_ATTN + N_ATTN_HEADS
DEEPNORM_ALPHA = (2.0 * DEPTH) ** 0.25
DEEPNORM_BETA = (8.0 * DEPTH) ** -0.25
LN_EPS = 1e-5
NEG_INF = -1e30

kernel_name = "hymba_rglru_fox_adaln_deepnorm"


def _layer_norm(x):
    x32 = x.astype(jnp.float32)
    mu = jnp.mean(x32, axis=-1, keepdims=True)
    var = jnp.mean(jnp.square(x32 - mu), axis=-1, keepdims=True)
    return (x32 - mu) * lax.rsqrt(var + LN_EPS)


def _rms_norm(y, g):
    y32 = y.astype(jnp.float32)
    return y32 * lax.rsqrt(jnp.mean(jnp.square(y32), axis=-1, keepdims=True) + LN_EPS) * g.astype(jnp.float32)


def _causal_depthwise_conv(x, w, b):
    K = w.shape[0]
    S = x.shape[1]
    xp = jnp.pad(x, ((0, 0), (K - 1, 0), (0, 0)))
    out = b
    for k in range(K):
        out = out + xp[:, k:k + S, :] * w[k]
    return out


def _rg_lru(x, w_a, b_a, w_x, b_x, lam):
    B, S, C = x.shape
    xb = x.reshape(B, S, LRU_BLOCKS, LRU_BLOCK_W)
    r = jax.nn.sigmoid((jnp.einsum('bsnd,nde->bsne', xb, w_a).reshape(B, S, C) + b_a).astype(jnp.float32))
    i = jax.nn.sigmoid((jnp.einsum('bsnd,nde->bsne', xb, w_x).reshape(B, S, C) + b_x).astype(jnp.float32))
    log_a = -LRU_C * r * jax.nn.softplus(-lam.astype(jnp.float32))
    a = jnp.exp(log_a)
    mult = jnp.sqrt(-jnp.expm1(2.0 * log_a))
    u = mult * (i * x.astype(jnp.float32))

    def combine(left, right):
        a1, b1 = left
        a2, b2 = right
        return a2 * a1, a2 * b1 + b2

    _, h = lax.associative_scan(combine, (a, u), axis=1)
    return h


def _forgetting_attention(q, k, v, log_f):
    B, S, _ = q.shape
    H, dh = N_ATTN_HEADS, HEAD_DIM
    q = q.reshape(B, S, H, dh).transpose(0, 2, 1, 3)
    k = k.reshape(B, S, H, dh).transpose(0, 2, 1, 3)
    v = v.reshape(B, S, H, dh).transpose(0, 2, 1, 3)
    d_cum = jnp.cumsum(log_f.astype(jnp.float32), axis=1).transpose(0, 2, 1)
    scale = dh ** -0.5
    kpos = jnp.arange(S)
    n_blocks = S // Q_BLOCK

    def block(bi):
        qs = bi * Q_BLOCK
        qb = lax.dynamic_slice_in_dim(q, qs, Q_BLOCK, axis=2)
        dq = lax.dynamic_slice_in_dim(d_cum, qs, Q_BLOCK, axis=2)
        s = jnp.einsum('bhqd,bhkd->bhqk', qb, k).astype(jnp.float32) * scale
        s = s + (dq[..., :, None] - d_cum[:, :, None, :])
        qpos = qs + jnp.arange(Q_BLOCK)
        mask = kpos[None, :] <= qpos[:, None]
        s = jnp.where(mask, s, NEG_INF)
        p = jax.nn.softmax(s, axis=-1)
        return jnp.einsum('bhqk,bhkd->bhqd', p.astype(v.dtype), v)

    out = lax.map(block, jnp.arange(n_blocks))
    return out.transpose(1, 0, 3, 2, 4).reshape(B, S, H * dh)


def setup_inputs(seed: int = 0) -> dict:
    key = jax.random.key(seed)
    ks = jax.random.split(key, 18)
    f32 = jnp.float32
    x = jax.random.normal(ks[0], (BATCH, SEQ, D_MODEL), f32)
    c = jax.random.normal(ks[1], (BATCH, D_MODEL), f32)
    w_ada = jax.random.normal(ks[2], (DEPTH, D_MODEL, 3 * D_MODEL), f32) * (0.5 * D_MODEL ** -0.5)
    b_ada = 0.01 * jax.random.normal(ks[3], (DEPTH, 3 * D_MODEL), f32)
    w_in = jax.random.normal(ks[4], (DEPTH, D_MODEL, N_IN), f32) * D_MODEL ** -0.5
    b_fgate = jax.random.uniform(ks[5], (DEPTH, N_ATTN_HEADS), f32, minval=1.0, maxval=5.0)
    conv_w = jax.random.normal(ks[6], (DEPTH, CONV_W, D_LRU), f32) * CONV_W ** -0.5
    conv_b = 0.01 * jax.random.normal(ks[7], (DEPTH, D_LRU), f32)
    w_gate_a = jax.random.normal(ks[8], (DEPTH, LRU_BLOCKS, LRU_BLOCK_W, LRU_BLOCK_W), f32) * LRU_BLOCK_W ** -0.5
    b_gate_a = 0.01 * jax.random.normal(ks[9], (DEPTH, D_LRU), f32)
    w_gate_x = jax.random.normal(ks[10], (DEPTH, LRU_BLOCKS, LRU_BLOCK_W, LRU_BLOCK_W), f32) * LRU_BLOCK_W ** -0.5
    b_gate_x = 0.01 * jax.random.normal(ks[11], (DEPTH, D_LRU), f32)
    u = jax.random.uniform(ks[12], (DEPTH, D_LRU), f32, minval=0.9, maxval=0.999)
    a0 = u ** (1.0 / LRU_C)
    lru_lambda = jnp.log(a0) - jnp.log1p(-a0)
    norm_lru = 1.0 + 0.01 * jax.random.normal(ks[13], (DEPTH, D_LRU), f32)
    norm_attn = 1.0 + 0.01 * jax.random.normal(ks[14], (DEPTH, D_ATTN), f32)
    w_out = jax.random.normal(ks[15], (DEPTH, D_MIX, D_MODEL), f32) * (D_MIX ** -0.5 * DEEPNORM_BETA)
    ln_gain = 1.0 + 0.01 * jax.random.normal(ks[16], (DEPTH, D_MODEL), f32)
    ln_bias = 0.01 * jax.random.normal(ks[17], (DEPTH, D_MODEL), f32)
    return {"x": x, "c": c, "w_ada": w_ada, "b_ada": b_ada, "w_in": w_in, "b_fgate": b_fgate,
            "conv_w": conv_w, "conv_b": conv_b, "w_gate_a": w_gate_a, "b_gate_a": b_gate_a,
            "w_gate_x": w_gate_x, "b_gate_x": b_gate_x, "lru_lambda": lru_lambda,
            "norm_lru": norm_lru, "norm_attn": norm_attn, "w_out": w_out,
            "ln_gain": ln_gain, "ln_bias": ln_bias}


def reference(x, c, w_ada, b_ada, w_in, b_fgate, conv_w, conv_b, w_gate_a, b_gate_a,
              w_gate_x, b_gate_x, lru_lambda, norm_lru, norm_attn, w_out, ln_gain, ln_bias):
    dtype = x.dtype
    split_idx = [D_LRU, 2 * D_LRU, 2 * D_LRU + D_ATTN, 2 * D_LRU + 2 * D_ATTN,
                 2 * D_LRU + 3 * D_ATTN, 2 * D_LRU + 4 * D_ATTN]
    c_act = jax.nn.silu(c)
    for l in range(DEPTH):
        mod = c_act @ w_ada[l] + b_ada[l]
        shift, scale, gate = jnp.split(mod, 3, axis=-1)
        h = (_layer_norm(x) * (1.0 + scale[:, None, :].astype(jnp.float32))
             + shift[:, None, :].astype(jnp.float32)).astype(dtype)
        proj = h @ w_in[l]
        xr, zr, q, k, v, za, fl = jnp.split(proj, split_idx, axis=-1)
        xc = _causal_depthwise_conv(xr, conv_w[l], conv_b[l])
        y_r = _rg_lru(xc, w_gate_a[l], b_gate_a[l], w_gate_x[l], b_gate_x[l], lru_lambda[l])
        y_r = (_rms_norm(y_r, norm_lru[l]) * jax.nn.silu(zr.astype(jnp.float32))).astype(dtype)
        log_f = jax.nn.log_sigmoid((fl + b_fgate[l]).astype(jnp.float32))
        y_a = _forgetting_attention(q, k, v, log_f)
        y_a = (_rms_norm(y_a, norm_attn[l]) * jax.nn.silu(za.astype(jnp.float32))).astype(dtype)
        y = jnp.concatenate([y_r, y_a], axis=-1) @ w_out[l]
        res = DEEPNORM_ALPHA * x.astype(jnp.float32) + gate[:, None, :].astype(jnp.float32) * y.astype(jnp.float32)
        x = (_layer_norm(res) * ln_gain[l].astype(jnp.float32) + ln_bias[l].astype(jnp.float32)).astype(dtype)
    return x
```

```python
import functools

import numpy as np
import jax
import jax.numpy as jnp
from jax import lax
from jax.experimental import pallas as pl
from jax.experimental.pallas import tpu as pltpu

F32 = jnp.float32
BF16 = jnp.bfloat16

D_MODEL = 1024
D_LRU = 512
D_ATTN = 512
N_HEADS = 8
HEAD_DIM = 64
N_PAIRS = N_HEADS // 2
LRU_BLOCKS = 8
CONV_W = 4
LRU_C = 8.0
LN_EPS = 1e-5
MASK_VALUE = -1e30
LOG2E = 1.4426950408889634

SUBLANES = 8
LANES = 128
BF16_ROWS = 16
V_ROWS = HEAD_DIM + BF16_ROWS

TOKEN_TILE = 512
Q_TILE = 512
KV_TILE = TOKEN_TILE
VMEM_LIMIT = 56 * 1024 * 1024


def _params(*sem):
    return pltpu.CompilerParams(dimension_semantics=sem, vmem_limit_bytes=VMEM_LIMIT)


def _adaln_kernel(c_ref, w_ref, b_ref, o_ref):
    c = c_ref[...]
    act = c * jax.nn.sigmoid(c)
    o_ref[...] = jnp.dot(act, w_ref[...], precision=lax.Precision.HIGHEST,
                         preferred_element_type=F32) + b_ref[...]


def _adaln(c8, w_ada, b_ada):
    depth, _, n = w_ada.shape
    bn = 1024
    return pl.pallas_call(
        _adaln_kernel,
        grid=(depth, n // bn),
        in_specs=[
            pl.BlockSpec((SUBLANES, D_MODEL), lambda l, j: (0, 0)),
            pl.BlockSpec((None, D_MODEL, bn), lambda l, j: (l, 0, j)),
            pl.BlockSpec((None, 1, bn), lambda l, j: (l, 0, j)),
        ],
        out_specs=pl.BlockSpec((None, SUBLANES, bn), lambda l, j: (l, 0, j)),
        out_shape=jax.ShapeDtypeStruct((depth, SUBLANES, n), F32),
        compiler_params=_params("arbitrary", "arbitrary"),
        name="adaln_mod",
    )(c8, w_ada, b_ada.reshape(depth, 1, n))


def _cumsum_rows(x):
    n = x.shape[0]
    row = lax.broadcasted_iota(jnp.int32, x.shape, 0)
    s = 1
    while s < n:
        x = x + jnp.where(row >= s, pltpu.roll(x, s, 0), 0.0)
        s *= 2
    return x


def _inproj_kernel(x_ref, scale_ref, shift_ref, w_ref, wvt_ref, wf_ref, bf_ref,
                   pq_ref, pk_ref,
                   xr_ref, zr_ref, za_ref, qx_ref, kx_ref, vt_ref, dcarry_ref):
    tm = x_ref.shape[0]

    @pl.when(pl.program_id(1) == 0)
    def _():
        dcarry_ref[...] = jnp.zeros_like(dcarry_ref)

    x = x_ref[...]
    mu = jnp.mean(x, axis=-1, keepdims=True)
    xc = x - mu
    var = jnp.mean(xc * xc, axis=-1, keepdims=True)
    h = xc * lax.rsqrt(var + LN_EPS) * (1.0 + scale_ref[...]) + shift_ref[...]
    hb = h.astype(BF16)

    def proj(lo):
        return jnp.dot(hb, w_ref[:, lo:lo + 512], preferred_element_type=F32)

    xr_ref[...] = proj(0)
    zr_ref[...] = proj(D_LRU)
    za_ref[...] = proj(2 * D_LRU + 3 * D_ATTN)

    fl = jnp.dot(hb, wf_ref[...], preferred_element_type=F32) + bf_ref[...]
    logf = jnp.minimum(fl, 0.0) - jnp.log1p(jnp.exp(-jnp.abs(fl)))
    d = _cumsum_rows(logf) + dcarry_ref[0:1, :]
    dcarry_ref[...] = jnp.broadcast_to(d[tm - 1:tm, :], dcarry_ref.shape)
    d2 = d * LOG2E
    hi = d2.astype(BF16).astype(F32)
    r1 = d2 - hi
    mid = r1.astype(BF16).astype(F32)
    lo = (r1 - mid).astype(BF16).astype(F32)
    lane = lax.broadcasted_iota(jnp.int32, d2.shape, 1)
    dsplit = jnp.where(lane < N_HEADS, hi,
             jnp.where(lane < 2 * N_HEADS, mid,
             jnp.where(lane < 3 * N_HEADS, lo,
             jnp.where(lane == 3 * N_HEADS, 1.0, 0.0)))).astype(BF16)
    qbias = jnp.dot(dsplit, pq_ref[...], preferred_element_type=F32)
    kbias = jnp.dot(dsplit, pk_ref[...], preferred_element_type=F32)

    q = proj(2 * D_LRU) * (HEAD_DIM ** -0.5 * LOG2E)
    k = proj(2 * D_LRU + D_ATTN)
    for p in range(N_PAIRS):
        src = slice(p * LANES, (p + 1) * LANES)
        qx_ref[:, 2 * p * LANES:(2 * p + 1) * LANES] = q[:, src].astype(BF16)
        qx_ref[:, (2 * p + 1) * LANES:(2 * p + 2) * LANES] = qbias[:, src].astype(BF16)
        kx_ref[:, 2 * p * LANES:(2 * p + 1) * LANES] = k[:, src].astype(BF16)
        kx_ref[:, (2 * p + 1) * LANES:(2 * p + 2) * LANES] = kbias[:, src].astype(BF16)

    vt = lax.dot_general(wvt_ref[...], hb, (((1,), (1,)), ((), ())),
                         preferred_element_type=F32)
    ones = jnp.ones((BF16_ROWS, tm), BF16)
    for hd in range(N_HEADS):
        vt_ref[hd * V_ROWS:hd * V_ROWS + HEAD_DIM, :] = (
            vt[hd * HEAD_DIM:(hd + 1) * HEAD_DIM, :].astype(BF16))
        vt_ref[hd * V_ROWS + HEAD_DIM:(hd + 1) * V_ROWS, :] = ones


def _inproj(x, scale, shift, w_main, wvt, wf, bf, pq, pk):
    b, s, _ = x.shape
    tm = TOKEN_TILE
    nt = s // tm
    tok = lambda w: pl.BlockSpec((None, tm, w), lambda i, t: (i, t, 0))
    full = lambda a: pl.BlockSpec(a.shape, lambda i, t: (0,) * a.ndim)
    vec = pl.BlockSpec((None, 1, D_MODEL), lambda i, t: (i, 0, 0))
    return pl.pallas_call(
        _inproj_kernel,
        grid=(b, nt),
        in_specs=[tok(D_MODEL), vec, vec, full(w_main), full(wvt), full(wf), full(bf),
                  full(pq), full(pk)],
        out_specs=[tok(D_LRU), tok(D_LRU), tok(D_ATTN), tok(2 * D_ATTN), tok(2 * D_ATTN),
                   pl.BlockSpec((None, None, N_HEADS * V_ROWS, tm), lambda i, t: (i, t, 0, 0))],
        out_shape=[jax.ShapeDtypeStruct((b, s, D_LRU), F32),
                   jax.ShapeDtypeStruct((b, s, D_LRU), F32),
                   jax.ShapeDtypeStruct((b, s, D_ATTN), F32),
                   jax.ShapeDtypeStruct((b, s, 2 * D_ATTN), BF16),
                   jax.ShapeDtypeStruct((b, s, 2 * D_ATTN), BF16),
                   jax.ShapeDtypeStruct((b, nt, N_HEADS * V_ROWS, tm), BF16)],
        scratch_shapes=[pltpu.VMEM((SUBLANES, LANES), F32)],
        compiler_params=_params("arbitrary", "arbitrary"),
        name="ln_inproj",
    )(x, scale, shift, w_main, wvt, wf, bf, pq, pk)


def _lru_kernel(xr_ref, zr_ref, cw_ref, cb_ref, wg_ref, bg_ref, lam_ref, g_ref,
                o_ref, xbuf, a_s, u_s, h7_s, p7_s, c_s, hcarry):
    tc = xr_ref.shape[0]
    ng = tc // SUBLANES

    @pl.when(pl.program_id(1) == 0)
    def _():
        xbuf[0:SUBLANES, :] = jnp.zeros((SUBLANES, D_LRU), F32)
        hcarry[...] = jnp.zeros_like(hcarry)

    xbuf[SUBLANES:SUBLANES + tc, :] = xr_ref[...]
    xc = cb_ref[...]
    for kk in range(CONV_W):
        off = SUBLANES - (CONV_W - 1) + kk
        xc = xc + cw_ref[kk:kk + 1, :] * xbuf[off:off + tc, :]
    xbuf[0:SUBLANES, :] = xbuf[tc:tc + SUBLANES, :]

    g = jnp.dot(xc.astype(BF16), wg_ref[...], preferred_element_type=F32) + bg_ref[...]
    r = jax.nn.sigmoid(g[:, :D_LRU])
    i = jax.nn.sigmoid(g[:, D_LRU:])
    nl = -lam_ref[...]
    sp = jnp.maximum(nl, 0.0) + jnp.log1p(jnp.exp(-jnp.abs(nl)))
    log_a = (-LRU_C) * r * sp
    a = jnp.exp(log_a)
    mult = jnp.sqrt(-jnp.tanh(log_a) * (1.0 + a * a))
    u = mult * (i * xc)
    n_slab = D_LRU // LANES
    for sl in range(n_slab):
        a_s[sl] = a[:, sl * LANES:(sl + 1) * LANES]
        u_s[sl] = u[:, sl * LANES:(sl + 1) * LANES]

    row = lambda j: pl.ds(j, ng, stride=SUBLANES)
    for sl in range(n_slab):
        hl = u_s[sl, row(0), :]
        pr = a_s[sl, row(0), :]
        for j in range(1, SUBLANES):
            aj = a_s[sl, row(j), :]
            hl = aj * hl + u_s[sl, row(j), :]
            pr = aj * pr
            u_s[sl, row(j), :] = hl
            a_s[sl, row(j), :] = pr
        h7_s[:, sl * LANES:(sl + 1) * LANES] = hl
        p7_s[:, sl * LANES:(sl + 1) * LANES] = pr

    def carry_step(gi, c):
        c_s[pl.ds(gi, 1), :] = c
        return p7_s[pl.ds(gi, 1), :] * c + h7_s[pl.ds(gi, 1), :]

    hcarry[0:1, :] = lax.fori_loop(0, ng, carry_step, hcarry[0:1, :])

    for sl in range(n_slab):
        cin = c_s[:, sl * LANES:(sl + 1) * LANES]
        for j in range(SUBLANES):
            u_s[sl, row(j), :] = u_s[sl, row(j), :] + a_s[sl, row(j), :] * cin

    hseq = jnp.concatenate([u_s[sl] for sl in range(n_slab)], axis=1)
    ms = jnp.mean(hseq * hseq, axis=-1, keepdims=True)
    z = zr_ref[...]
    y = hseq * lax.rsqrt(ms + LN_EPS) * g_ref[...] * (z * jax.nn.sigmoid(z))
    o_ref[...] = y.astype(o_ref.dtype)


def _lru(xr, zr, cw, cb, wg, bg, lam, gain):
    b, s, _ = xr.shape
    tc = TOKEN_TILE
    tok = pl.BlockSpec((None, tc, D_LRU), lambda i, t: (i, t, 0))
    full = lambda a: pl.BlockSpec(a.shape, lambda i, t: (0,) * a.ndim)
    ng = tc // SUBLANES
    return pl.pallas_call(
        _lru_kernel,
        grid=(b, s // tc),
        in_specs=[tok, tok, full(cw), full(cb), full(wg), full(bg), full(lam), full(gain)],
        out_specs=tok,
        out_shape=jax.ShapeDtypeStruct((b, s, D_LRU), BF16),
        scratch_shapes=[pltpu.VMEM((tc + SUBLANES, D_LRU), F32),
                        pltpu.VMEM((D_LRU // LANES, tc, LANES), F32),
                        pltpu.VMEM((D_LRU // LANES, tc, LANES), F32),
                        pltpu.VMEM((ng, D_LRU), F32),
                        pltpu.VMEM((ng, D_LRU), F32),
                        pltpu.VMEM((ng, D_LRU), F32),
                        pltpu.VMEM((SUBLANES, D_LRU), F32)],
        compiler_params=_params("arbitrary", "arbitrary"),
        name="conv_rglru",
    )(xr, zr, cw, cb, wg, bg, lam, gain)


_NT = (((1,), (1,)), ((), ()))


def _attn_kernel(q_ref, k_ref, v_ref, o_ref, acc_ref, m_ref):
    tq = q_ref.shape[0]
    tk = v_ref.shape[-1]
    qi = pl.program_id(2)

    qf = q_ref[...].astype(F32)
    lane = lax.broadcasted_iota(jnp.int32, qf.shape, 1)
    nb = 6
    in_a = (lane < HEAD_DIM) | ((lane >= 2 * HEAD_DIM) & (lane < 2 * HEAD_DIM + nb))
    in_b = (((lane >= HEAD_DIM) & (lane < 2 * HEAD_DIM))
            | ((lane >= 2 * HEAD_DIM + nb) & (lane < 2 * HEAD_DIM + 2 * nb)))
    qh = (jnp.where(in_a, qf, 0.0).astype(BF16), jnp.where(in_b, qf, 0.0).astype(BF16))

    m_ref[...] = jnp.full(m_ref.shape, MASK_VALUE, F32)
    acc_ref[...] = jnp.zeros_like(acc_ref)

    def step(j, masked):
        k = k_ref[pl.ds(pl.multiple_of(j * tk, tk), tk), :]
        for hd in range(2):
            s = lax.dot_general(k, qh[hd], _NT, preferred_element_type=F32)
            if masked:
                kpos = j * tk + lax.broadcasted_iota(jnp.int32, s.shape, 0)
                qpos = qi * tq + lax.broadcasted_iota(jnp.int32, s.shape, 1)
                s = jnp.where(kpos <= qpos, s, MASK_VALUE)
            m_old = m_ref[hd]
            m_new = jnp.maximum(m_old, jnp.max(s, axis=0, keepdims=True))
            alpha = jnp.exp2(m_old - m_new)
            p = jnp.exp2(s - m_new).astype(BF16)
            vt = v_ref[j, hd * V_ROWS:(hd + 1) * V_ROWS, :]
            acc_ref[hd] = alpha * acc_ref[hd] + jnp.dot(vt, p, preferred_element_type=F32)
            m_ref[hd] = m_new

    def full_step(j, carry):
        step(j, False)
        return carry

    lax.fori_loop(0, qi, full_step, 0)
    step(qi, True)

    outs = []
    for hd in range(2):
        acc = acc_ref[hd]
        outs.append(acc[:HEAD_DIM, :] / acc[HEAD_DIM:HEAD_DIM + 1, :])
    o_ref[...] = jnp.concatenate(outs, axis=0).T


def _attention(qx, kx, vt):
    b, s, _ = qx.shape
    nkv, _, tk = vt.shape[1:]
    tq = Q_TILE
    assert tq == tk
    return pl.pallas_call(
        _attn_kernel,
        grid=(b, N_PAIRS, s // tq),
        in_specs=[
            pl.BlockSpec((None, tq, 2 * LANES), lambda i, p, t: (i, t, p)),
            pl.BlockSpec((None, s, 2 * LANES), lambda i, p, t: (i, 0, p)),
            pl.BlockSpec((None, nkv, 2 * V_ROWS, tk), lambda i, p, t: (i, 0, p, 0)),
        ],
        out_specs=pl.BlockSpec((None, tq, LANES), lambda i, p, t: (i, t, p)),
        out_shape=jax.ShapeDtypeStruct((b, s, D_ATTN), F32),
        scratch_shapes=[pltpu.VMEM((2, V_ROWS, tq), F32),
                        pltpu.VMEM((2, 1, tq), F32)],
        compiler_params=_params("arbitrary", "arbitrary", "arbitrary"),
        name="fox_attention",
    )(qx, kx, vt)


def _outproj_kernel(alpha, yr_ref, ya_ref, za_ref, x_ref, gate_ref, na_ref, w_ref,
                    g_ref, b_ref, o_ref):
    ya = ya_ref[...]
    ms = jnp.mean(ya * ya, axis=-1, keepdims=True)
    za = za_ref[...]
    ya = ya * lax.rsqrt(ms + LN_EPS) * na_ref[...] * (za * jax.nn.sigmoid(za))
    y = (jnp.dot(yr_ref[...], w_ref[0:D_LRU, :], preferred_element_type=F32)
         + jnp.dot(ya.astype(BF16), w_ref[D_LRU:, :], preferred_element_type=F32))
    res = alpha * x_ref[...] + gate_ref[...] * y
    mu = jnp.mean(res, axis=-1, keepdims=True)
    rc = res - mu
    var = jnp.mean(rc * rc, axis=-1, keepdims=True)
    o_ref[...] = rc * lax.rsqrt(var + LN_EPS) * g_ref[...] + b_ref[...]


def _outproj(alpha, yr, ya, za, x, gate, na, w, g, bb):
    b, s, _ = x.shape
    tm = TOKEN_TILE
    tok = lambda w_: pl.BlockSpec((None, tm, w_), lambda i, t: (i, t, 0))
    full = lambda a: pl.BlockSpec(a.shape, lambda i, t: (0,) * a.ndim)
    vec = pl.BlockSpec((None, 1, D_MODEL), lambda i, t: (i, 0, 0))
    return pl.pallas_call(
        functools.partial(_outproj_kernel, alpha),
        grid=(b, s // tm),
        in_specs=[tok(D_LRU), tok(D_ATTN), tok(D_ATTN), tok(D_MODEL), vec, full(na),
                  full(w), full(g), full(bb)],
        out_specs=tok(D_MODEL),
        out_shape=jax.ShapeDtypeStruct((b, s, D_MODEL), F32),
        compiler_params=_params("arbitrary", "arbitrary"),
        name="outproj_ln",
    )(yr, ya, za, x, gate, na, w, g, bb)


def _bias_placement():
    pq = np.zeros((LANES, D_ATTN), np.float32)
    pk = np.zeros((LANES, D_ATTN), np.float32)
    one = 3 * N_HEADS
    for hd in range(N_HEADS):
        base = (hd // 2) * LANES + (hd % 2) * 6
        for part in range(3):
            src = part * N_HEADS + hd
            pq[src, base + part] = 1.0
            pq[one, base + 3 + part] = 1.0
            pk[one, base + part] = 1.0
            pk[src, base + 3 + part] = -1.0
    return jnp.asarray(pq, BF16), jnp.asarray(pk, BF16)


def _block_diag(w):
    nb, bw, _ = w.shape
    eye = jnp.eye(nb, dtype=w.dtype)
    return (eye[:, None, :, None] * w[:, :, None, :]).reshape(nb * bw, nb * bw)


def kernel(x, c, w_ada, b_ada, w_in, b_fgate, conv_w, conv_b, w_gate_a, b_gate_a,
           w_gate_x, b_gate_x, lru_lambda, norm_lru, norm_attn, w_out, ln_gain, ln_bias):
    depth = w_ada.shape[0]
    batch = x.shape[0]
    alpha = (2.0 * depth) ** 0.25
    n_main = 2 * D_LRU + 4 * D_ATTN

    c8 = jnp.pad(c, ((0, SUBLANES - batch), (0, 0)))
    mod = _adaln(c8, w_ada, b_ada)[:, :batch, :]
    pq, pk = _bias_placement()

    for l in range(depth):
        shift = mod[l, :, 0:D_MODEL].reshape(batch, 1, D_MODEL)
        scale = mod[l, :, D_MODEL:2 * D_MODEL].reshape(batch, 1, D_MODEL)
        gate = mod[l, :, 2 * D_MODEL:].reshape(batch, 1, D_MODEL)

        w_main = w_in[l, :, :n_main].astype(BF16)
        wvt = w_in[l, :, 2 * D_LRU + 2 * D_ATTN:2 * D_LRU + 3 * D_ATTN].T.astype(BF16)
        wf = jnp.zeros((D_MODEL, LANES), F32).at[:, :3 * N_HEADS].set(
            jnp.tile(w_in[l, :, n_main:], (1, 3))).astype(BF16)
        bf = jnp.zeros((1, LANES), F32).at[0, :3 * N_HEADS].set(jnp.tile(b_fgate[l], 3))
        xr, zr, za, qx, kx, vt = _inproj(x, scale, shift, w_main, wvt, wf, bf, pq, pk)

        wg = jnp.concatenate([_block_diag(w_gate_a[l]), _block_diag(w_gate_x[l])],
                             axis=1).astype(BF16)
        bg = jnp.concatenate([b_gate_a[l], b_gate_x[l]]).reshape(1, 2 * D_LRU)
        yr = _lru(xr, zr, conv_w[l], conv_b[l].reshape(1, D_LRU), wg, bg,
                  lru_lambda[l].reshape(1, D_LRU), norm_lru[l].reshape(1, D_LRU))

        ya = _attention(qx, kx, vt)

        x = _outproj(alpha, yr, ya, za, x, gate, norm_attn[l].reshape(1, D_ATTN),
                     w_out[l].astype(BF16), ln_gain[l].reshape(1, D_MODEL),
                     ln_bias[l].reshape(1, D_MODEL))
    return x
```

```python
import functools

import numpy as np
import jax
import jax.numpy as jnp
from jax import lax
from jax.experimental import pallas as pl
from jax.experimental.pallas import tpu as pltpu

F32 = jnp.float32
BF16 = jnp.bfloat16

D_MODEL = 1024
D_LRU = 512
D_ATTN = 512
N_HEADS = 8
HEAD_DIM = 64
N_PAIRS = N_HEADS // 2
LRU_BLOCKS = 8
CONV_W = 4
LRU_C = 8.0
LN_EPS = 1e-5
MASK_VALUE = -1e30
LOG2E = 1.4426950408889634

SUBLANES = 8
LANES = 128
BF16_ROWS = 16
V_ROWS = HEAD_DIM + BF16_ROWS

TOKEN_TILE = 512
Q_TILE = 1024
KV_TILE = TOKEN_TILE
VMEM_LIMIT = 56 * 1024 * 1024


def _params(*sem):
    return pltpu.CompilerParams(dimension_semantics=sem, vmem_limit_bytes=VMEM_LIMIT)


def _adaln_kernel(c_ref, w_ref, b_ref, o_ref):
    c = c_ref[...]
    act = c * jax.nn.sigmoid(c)
    o_ref[...] = jnp.dot(act, w_ref[...], precision=lax.Precision.HIGHEST,
                         preferred_element_type=F32) + b_ref[...]


def _adaln(c8, w_ada, b_ada):
    depth, _, n = w_ada.shape
    bn = 1024
    return pl.pallas_call(
        _adaln_kernel,
        grid=(depth, n // bn),
        in_specs=[
            pl.BlockSpec((SUBLANES, D_MODEL), lambda l, j: (0, 0)),
            pl.BlockSpec((None, D_MODEL, bn), lambda l, j: (l, 0, j)),
            pl.BlockSpec((None, 1, bn), lambda l, j: (l, 0, j)),
        ],
        out_specs=pl.BlockSpec((None, SUBLANES, bn), lambda l, j: (l, 0, j)),
        out_shape=jax.ShapeDtypeStruct((depth, SUBLANES, n), F32),
        compiler_params=_params("arbitrary", "arbitrary"),
        name="adaln_mod",
    )(c8, w_ada, b_ada.reshape(depth, 1, n))


def _cumsum_rows(x):
    n = x.shape[0]
    row = lax.broadcasted_iota(jnp.int32, x.shape, 0)
    s = 1
    while s < n:
        x = x + jnp.where(row >= s, pltpu.roll(x, s, 0), 0.0)
        s *= 2
    return x


def _inproj_kernel(x_ref, scale_ref, shift_ref, w_ref, wvt_ref, wf_ref, bf_ref,
                   pq_ref, pk_ref,
                   xr_ref, zr_ref, za_ref, qx_ref, kx_ref, vt_ref, dcarry_ref):
    tm = x_ref.shape[0]

    @pl.when(pl.program_id(1) == 0)
    def _():
        dcarry_ref[...] = jnp.zeros_like(dcarry_ref)

    x = x_ref[...]
    mu = jnp.mean(x, axis=-1, keepdims=True)
    xc = x - mu
    var = jnp.mean(xc * xc, axis=-1, keepdims=True)
    h = xc * lax.rsqrt(var + LN_EPS) * (1.0 + scale_ref[...]) + shift_ref[...]
    hb = h.astype(BF16)

    def proj(lo):
        return jnp.dot(hb, w_ref[:, lo:lo + 512], preferred_element_type=F32)

    xr_ref[...] = proj(0)
    zr_ref[...] = proj(D_LRU)
    za_ref[...] = proj(2 * D_LRU + 3 * D_ATTN)

    fl = jnp.dot(hb, wf_ref[...], preferred_element_type=F32) + bf_ref[...]
    logf = jnp.minimum(fl, 0.0) - jnp.log1p(jnp.exp(-jnp.abs(fl)))
    d = _cumsum_rows(logf) + dcarry_ref[0:1, :]
    dcarry_ref[...] = jnp.broadcast_to(d[tm - 1:tm, :], dcarry_ref.shape)
    d2 = d * LOG2E
    hi = d2.astype(BF16).astype(F32)
    r1 = d2 - hi
    mid = r1.astype(BF16).astype(F32)
    lo = (r1 - mid).astype(BF16).astype(F32)
    lane = lax.broadcasted_iota(jnp.int32, d2.shape, 1)
    dsplit = jnp.where(lane < N_HEADS, hi,
             jnp.where(lane < 2 * N_HEADS, mid,
             jnp.where(lane < 3 * N_HEADS, lo,
             jnp.where(lane == 3 * N_HEADS, 1.0, 0.0)))).astype(BF16)
    qbias = jnp.dot(dsplit, pq_ref[...], preferred_element_type=F32)
    kbias = jnp.dot(dsplit, pk_ref[...], preferred_element_type=F32)

    q = proj(2 * D_LRU) * (HEAD_DIM ** -0.5 * LOG2E)
    k = proj(2 * D_LRU + D_ATTN)
    for p in range(N_PAIRS):
        src = slice(p * LANES, (p + 1) * LANES)
        qx_ref[:, 2 * p * LANES:(2 * p + 1) * LANES] = q[:, src].astype(BF16)
        qx_ref[:, (2 * p + 1) * LANES:(2 * p + 2) * LANES] = qbias[:, src].astype(BF16)
        kx_ref[:, 2 * p * LANES:(2 * p + 1) * LANES] = k[:, src].astype(BF16)
        kx_ref[:, (2 * p + 1) * LANES:(2 * p + 2) * LANES] = kbias[:, src].astype(BF16)

    vt = lax.dot_general(wvt_ref[...], hb, (((1,), (1,)), ((), ())),
                         preferred_element_type=F32)
    ones = jnp.ones((BF16_ROWS, tm), BF16)
    for hd in range(N_HEADS):
        vt_ref[hd * V_ROWS:hd * V_ROWS + HEAD_DIM, :] = (
            vt[hd * HEAD_DIM:(hd + 1) * HEAD_DIM, :].astype(BF16))
        vt_ref[hd * V_ROWS + HEAD_DIM:(hd + 1) * V_ROWS, :] = ones


def _inproj(x, scale, shift, w_main, wvt, wf, bf, pq, pk):
    b, s, _ = x.shape
    tm = TOKEN_TILE
    nt = s // tm
    tok = lambda w: pl.BlockSpec((None, tm, w), lambda i, t: (i, t, 0))
    full = lambda a: pl.BlockSpec(a.shape, lambda i, t: (0,) * a.ndim)
    vec = pl.BlockSpec((None, 1, D_MODEL), lambda i, t: (i, 0, 0))
    return pl.pallas_call(
        _inproj_kernel,
        grid=(b, nt),
        in_specs=[tok(D_MODEL), vec, vec, full(w_main), full(wvt), full(wf), full(bf),
                  full(pq), full(pk)],
        out_specs=[tok(D_LRU), tok(D_LRU), tok(D_ATTN), tok(2 * D_ATTN), tok(2 * D_ATTN),
                   pl.BlockSpec((None, None, N_HEADS * V_ROWS, tm), lambda i, t: (i, t, 0, 0))],
        out_shape=[jax.ShapeDtypeStruct((b, s, D_LRU), F32),
                   jax.ShapeDtypeStruct((b, s, D_LRU), F32),
                   jax.ShapeDtypeStruct((b, s, D_ATTN), F32),
                   jax.ShapeDtypeStruct((b, s, 2 * D_ATTN), BF16),
                   jax.ShapeDtypeStruct((b, s, 2 * D_ATTN), BF16),
                   jax.ShapeDtypeStruct((b, nt, N_HEADS * V_ROWS, tm), BF16)],
        scratch_shapes=[pltpu.VMEM((SUBLANES, LANES), F32)],
        compiler_params=_params("arbitrary", "arbitrary"),
        name="ln_inproj",
    )(x, scale, shift, w_main, wvt, wf, bf, pq, pk)


def _lru_kernel(xr_ref, zr_ref, cw_ref, cb_ref, wg_ref, bg_ref, lam_ref, g_ref,
                o_ref, xbuf, a_s, u_s, h7_s, p7_s, c_s, hcarry):
    tc = xr_ref.shape[0]
    ng = tc // SUBLANES

    @pl.when(pl.program_id(1) == 0)
    def _():
        xbuf[0:SUBLANES, :] = jnp.zeros((SUBLANES, D_LRU), F32)
        hcarry[...] = jnp.zeros_like(hcarry)

    xbuf[SUBLANES:SUBLANES + tc, :] = xr_ref[...]
    xc = cb_ref[...]
    for kk in range(CONV_W):
        off = SUBLANES - (CONV_W - 1) + kk
        xc = xc + cw_ref[kk:kk + 1, :] * xbuf[off:off + tc, :]
    xbuf[0:SUBLANES, :] = xbuf[tc:tc + SUBLANES, :]

    g = jnp.dot(xc.astype(BF16), wg_ref[...], preferred_element_type=F32) + bg_ref[...]
    r = jax.nn.sigmoid(g[:, :D_LRU])
    i = jax.nn.sigmoid(g[:, D_LRU:])
    nl = -lam_ref[...]
    sp = jnp.maximum(nl, 0.0) + jnp.log1p(jnp.exp(-jnp.abs(nl)))
    log_a = (-LRU_C) * r * sp
    a = jnp.exp(log_a)
    mult = jnp.sqrt(-jnp.tanh(log_a) * (1.0 + a * a))
    u = mult * (i * xc)
    n_slab = D_LRU // LANES
    for sl in range(n_slab):
        a_s[sl] = a[:, sl * LANES:(sl + 1) * LANES]
        u_s[sl] = u[:, sl * LANES:(sl + 1) * LANES]

    row = lambda j: pl.ds(j, ng, stride=SUBLANES)
    for sl in range(n_slab):
        hl = u_s[sl, row(0), :]
        pr = a_s[sl, row(0), :]
        for j in range(1, SUBLANES):
            aj = a_s[sl, row(j), :]
            hl = aj * hl + u_s[sl, row(j), :]
            pr = aj * pr
            u_s[sl, row(j), :] = hl
            a_s[sl, row(j), :] = pr
        h7_s[:, sl * LANES:(sl + 1) * LANES] = hl
        p7_s[:, sl * LANES:(sl + 1) * LANES] = pr

    def carry_step(gi, c):
        c_s[pl.ds(gi, 1), :] = c
        return p7_s[pl.ds(gi, 1), :] * c + h7_s[pl.ds(gi, 1), :]

    hcarry[0:1, :] = lax.fori_loop(0, ng, carry_step, hcarry[0:1, :])

    for sl in range(n_slab):
        cin = c_s[:, sl * LANES:(sl + 1) * LANES]
        for j in range(SUBLANES):
            u_s[sl, row(j), :] = u_s[sl, row(j), :] + a_s[sl, row(j), :] * cin

    hseq = jnp.concatenate([u_s[sl] for sl in range(n_slab)], axis=1)
    ms = jnp.mean(hseq * hseq, axis=-1, keepdims=True)
    z = zr_ref[...]
    y = hseq * lax.rsqrt(ms + LN_EPS) * g_ref[...] * (z * jax.nn.sigmoid(z))
    o_ref[...] = y.astype(o_ref.dtype)


def _lru(xr, zr, cw, cb, wg, bg, lam, gain):
    b, s, _ = xr.shape
    tc = TOKEN_TILE
    tok = pl.BlockSpec((None, tc, D_LRU), lambda i, t: (i, t, 0))
    full = lambda a: pl.BlockSpec(a.shape, lambda i, t: (0,) * a.ndim)
    ng = tc // SUBLANES
    return pl.pallas_call(
        _lru_kernel,
        grid=(b, s // tc),
        in_specs=[tok, tok, full(cw), full(cb), full(wg), full(bg), full(lam), full(gain)],
        out_specs=tok,
        out_shape=jax.ShapeDtypeStruct((b, s, D_LRU), BF16),
        scratch_shapes=[pltpu.VMEM((tc + SUBLANES, D_LRU), F32),
                        pltpu.VMEM((D_LRU // LANES, tc, LANES), F32),
                        pltpu.VMEM((D_LRU // LANES, tc, LANES), F32),
                        pltpu.VMEM((ng, D_LRU), F32),
                        pltpu.VMEM((ng, D_LRU), F32),
                        pltpu.VMEM((ng, D_LRU), F32),
                        pltpu.VMEM((SUBLANES, D_LRU), F32)],
        compiler_params=_params("arbitrary", "arbitrary"),
        name="conv_rglru",
    )(xr, zr, cw, cb, wg, bg, lam, gain)


_NT = (((1,), (1,)), ((), ()))


def _attn_kernel(q_ref, k_ref, v_ref, o_ref, q_s, q_n, s_a, s_b, mx_a, mx_b, acc_ref, m_ref):
    tq = o_ref.shape[0]
    tk = tq // 2
    qi = pl.program_id(2)
    nq = pl.num_programs(2)

    def load_q(t, dst):
        qf = q_ref[pl.ds(pl.multiple_of(t * tq, tq), tq), :].astype(F32)
        lane = lax.broadcasted_iota(jnp.int32, qf.shape, 1)
        nb = 6
        in_a = (lane < HEAD_DIM) | ((lane >= 2 * HEAD_DIM) & (lane < 2 * HEAD_DIM + nb))
        in_b = (((lane >= HEAD_DIM) & (lane < 2 * HEAD_DIM))
                | ((lane >= 2 * HEAD_DIM + nb) & (lane < 2 * HEAD_DIM + 2 * nb)))
        dst[0] = jnp.where(in_a, qf, 0.0).astype(BF16)
        dst[1] = jnp.where(in_b, qf, 0.0).astype(BF16)

    bufs = ((s_a, mx_a), (s_b, mx_b))

    def first_col(half, diag):
        return half * tk if diag else 0

    def scores(kv, half, diag, q_src=q_s):
        s_ref, mx_ref = bufs[half]
        c0 = first_col(half, diag)
        start = pl.multiple_of(kv * tq + half * tk, tk)
        k = k_ref[pl.ds(start, tk), :]
        for hd in range(2):
            s = lax.dot_general(k, q_src[hd, c0:, :], _NT, preferred_element_type=F32)
            if diag:
                kpos = half * tk + lax.broadcasted_iota(jnp.int32, s.shape, 0)
                qpos = c0 + lax.broadcasted_iota(jnp.int32, s.shape, 1)
                s = jnp.where(kpos <= qpos, s, MASK_VALUE)
            s_ref[hd, :, c0:] = s
            mx_ref[hd, :, c0:] = jnp.max(s, axis=0, keepdims=True)

    def consume(kv, half, diag=False):
        s_ref, mx_ref = bufs[half]
        c0 = first_col(half, diag)
        for hd in range(2):
            m_old = m_ref[hd, :, c0:]
            m_new = jnp.maximum(m_old, mx_ref[hd, :, c0:])
            alpha = jnp.exp2(m_old - m_new)
            p = jnp.exp2(s_ref[hd, :, c0:] - m_new).astype(BF16)
            rows = slice(hd * V_ROWS, (hd + 1) * V_ROWS)
            if v_ref.shape[-1] == tk:
                vt = v_ref[2 * kv + half, rows, :]
            else:
                vt = v_ref[kv, rows, half * tk:(half + 1) * tk]
            acc_ref[hd, :, c0:] = (alpha * acc_ref[hd, :, c0:]
                                   + jnp.dot(vt, p, preferred_element_type=F32))
            m_ref[hd, :, c0:] = m_new

    m_ref[...] = jnp.full(m_ref.shape, MASK_VALUE, F32)
    acc_ref[...] = jnp.zeros_like(acc_ref)

    load_q(qi, q_s)

    @pl.when(qi == 0)
    def _():
        scores(0, 0, True)

    def full_step(kv, carry):
        scores(kv, 1, False)
        consume(kv, 0)
        scores(kv + 1, 0, False)
        consume(kv, 1)
        return carry

    lax.fori_loop(0, qi - 1, full_step, 0)

    @pl.when(qi > 0)
    def _():
        scores(qi - 1, 1, False)
        consume(qi - 1, 0)
        scores(qi, 0, True)
        consume(qi - 1, 1)

    def drain(prefetch_next):
        scores(qi, 1, True)
        consume(qi, 0)
        if prefetch_next:
            load_q(qi + 1, q_n)
            scores(0, 0, False, q_n)
        consume(qi, 1, True)

    @pl.when(qi + 1 < nq)
    def _():
        drain(True)

    @pl.when(qi + 1 == nq)
    def _():
        drain(False)

    outs = []
    for hd in range(2):
        acc = acc_ref[hd]
        outs.append(acc[:HEAD_DIM, :] / acc[HEAD_DIM:HEAD_DIM + 1, :])
    o_ref[...] = jnp.concatenate(outs, axis=0).T


def _attention(qx, kx, vt):
    b, s, _ = qx.shape
    nkv, _, tk = vt.shape[1:]
    tq = Q_TILE
    assert tq // 2 in (tk, tk // 2)
    return pl.pallas_call(
        _attn_kernel,
        grid=(b, N_PAIRS, s // tq),
        in_specs=[
            pl.BlockSpec((None, s, 2 * LANES), lambda i, p, t: (i, 0, p)),
            pl.BlockSpec((None, s, 2 * LANES), lambda i, p, t: (i, 0, p)),
            pl.BlockSpec((None, nkv, 2 * V_ROWS, tk), lambda i, p, t: (i, 0, p, 0)),
        ],
        out_specs=pl.BlockSpec((None, tq, LANES), lambda i, p, t: (i, t, p)),
        out_shape=jax.ShapeDtypeStruct((b, s, D_ATTN), F32),
        scratch_shapes=[pltpu.VMEM((2, tq, 2 * LANES), BF16),
                        pltpu.VMEM((2, tq, 2 * LANES), BF16),
                        pltpu.VMEM((2, tq // 2, tq), F32),
                        pltpu.VMEM((2, tq // 2, tq), F32),
                        pltpu.VMEM((2, 1, tq), F32),
                        pltpu.VMEM((2, 1, tq), F32),
                        pltpu.VMEM((2, V_ROWS, tq), F32),
                        pltpu.VMEM((2, 1, tq), F32)],
        compiler_params=_params("arbitrary", "arbitrary", "arbitrary"),
        name="fox_attention",
    )(qx, kx, vt)


def _outproj_kernel(alpha, yr_ref, ya_ref, za_ref, x_ref, gate_ref, na_ref, w_ref,
                    g_ref, b_ref, o_ref):
    ya = ya_ref[...]
    ms = jnp.mean(ya * ya, axis=-1, keepdims=True)
    za = za_ref[...]
    ya = ya * lax.rsqrt(ms + LN_EPS) * na_ref[...] * (za * jax.nn.sigmoid(za))
    y = (jnp.dot(yr_ref[...], w_ref[0:D_LRU, :], preferred_element_type=F32)
         + jnp.dot(ya.astype(BF16), w_ref[D_LRU:, :], preferred_element_type=F32))
    res = alpha * x_ref[...] + gate_ref[...] * y
    mu = jnp.mean(res, axis=-1, keepdims=True)
    rc = res - mu
    var = jnp.mean(rc * rc, axis=-1, keepdims=True)
    o_ref[...] = rc * lax.rsqrt(var + LN_EPS) * g_ref[...] + b_ref[...]


def _outproj(alpha, yr, ya, za, x, gate, na, w, g, bb):
    b, s, _ = x.shape
    tm = TOKEN_TILE
    tok = lambda w_: pl.BlockSpec((None, tm, w_), lambda i, t: (i, t, 0))
    full = lambda a: pl.BlockSpec(a.shape, lambda i, t: (0,) * a.ndim)
    vec = pl.BlockSpec((None, 1, D_MODEL), lambda i, t: (i, 0, 0))
    return pl.pallas_call(
        functools.partial(_outproj_kernel, alpha),
        grid=(b, s // tm),
        in_specs=[tok(D_LRU), tok(D_ATTN), tok(D_ATTN), tok(D_MODEL), vec, full(na),
                  full(w), full(g), full(bb)],
        out_specs=tok(D_MODEL),
        out_shape=jax.ShapeDtypeStruct((b, s, D_MODEL), F32),
        compiler_params=_params("arbitrary", "arbitrary"),
        name="outproj_ln",
    )(yr, ya, za, x, gate, na, w, g, bb)


def _bias_placement():
    pq = np.zeros((LANES, D_ATTN), np.float32)
    pk = np.zeros((LANES, D_ATTN), np.float32)
    one = 3 * N_HEADS
    for hd in range(N_HEADS):
        base = (hd // 2) * LANES + (hd % 2) * 6
        for part in range(3):
            src = part * N_HEADS + hd
            pq[src, base + part] = 1.0
            pq[one, base + 3 + part] = 1.0
            pk[one, base + part] = 1.0
            pk[src, base + 3 + part] = -1.0
    return jnp.asarray(pq, BF16), jnp.asarray(pk, BF16)


def _block_diag(w):
    nb, bw, _ = w.shape
    eye = jnp.eye(nb, dtype=w.dtype)
    return (eye[:, None, :, None] * w[:, :, None, :]).reshape(nb * bw, nb * bw)


def kernel(x, c, w_ada, b_ada, w_in, b_fgate, conv_w, conv_b, w_gate_a, b_gate_a,
           w_gate_x, b_gate_x, lru_lambda, norm_lru, norm_attn, w_out, ln_gain, ln_bias):
    depth = w_ada.shape[0]
    batch = x.shape[0]
    alpha = (2.0 * depth) ** 0.25
    n_main = 2 * D_LRU + 4 * D_ATTN

    c8 = jnp.pad(c, ((0, SUBLANES - batch), (0, 0)))
    mod = _adaln(c8, w_ada, b_ada)[:, :batch, :]
    pq, pk = _bias_placement()

    for l in range(depth):
        shift = mod[l, :, 0:D_MODEL].reshape(batch, 1, D_MODEL)
        scale = mod[l, :, D_MODEL:2 * D_MODEL].reshape(batch, 1, D_MODEL)
        gate = mod[l, :, 2 * D_MODEL:].reshape(batch, 1, D_MODEL)

        w_main = w_in[l, :, :n_main].astype(BF16)
        wvt = w_in[l, :, 2 * D_LRU + 2 * D_ATTN:2 * D_LRU + 3 * D_ATTN].T.astype(BF16)
        wf = jnp.zeros((D_MODEL, LANES), F32).at[:, :3 * N_HEADS].set(
            jnp.tile(w_in[l, :, n_main:], (1, 3))).astype(BF16)
        bf = jnp.zeros((1, LANES), F32).at[0, :3 * N_HEADS].set(jnp.tile(b_fgate[l], 3))
        xr, zr, za, qx, kx, vt = _inproj(x, scale, shift, w_main, wvt, wf, bf, pq, pk)

        wg = jnp.concatenate([_block_diag(w_gate_a[l]), _block_diag(w_gate_x[l])],
                             axis=1).astype(BF16)
        bg = jnp.concatenate([b_gate_a[l], b_gate_x[l]]).reshape(1, 2 * D_LRU)
        yr = _lru(xr, zr, conv_w[l], conv_b[l].reshape(1, D_LRU), wg, bg,
                  lru_lambda[l].reshape(1, D_LRU), norm_lru[l].reshape(1, D_LRU))

        ya = _attention(qx, kx, vt)

        x = _outproj(alpha, yr, ya, za, x, gate, norm_attn[l].reshape(1, D_ATTN),
                     w_out[l].astype(BF16), ln_gain[l].reshape(1, D_MODEL),
                     ln_bias[l].reshape(1, D_MODEL))
    return x
```

```python
import functools

import numpy as np
import jax
import jax.numpy as jnp
from jax import lax
from jax.experimental import pallas as pl
from jax.experimental.pallas import tpu as pltpu

F32 = jnp.float32
BF16 = jnp.bfloat16

D_MODEL = 1024
D_LRU = 512
D_ATTN = 512
N_HEADS = 8
HEAD_DIM = 64
N_PAIRS = N_HEADS // 2
LRU_BLOCKS = 8
CONV_W = 4
LRU_C = 8.0
LN_EPS = 1e-5
MASK_VALUE = -1e30
TINY = 1e-37
LOG2E = 1.4426950408889634

SUBLANES = 8
LANES = 128
BF16_ROWS = 16
V_ROWS = HEAD_DIM + BF16_ROWS

TOKEN_TILE = 512
Q_TILE = 1024
KV_TILE = TOKEN_TILE
VMEM_LIMIT = 56 * 1024 * 1024


def _params(*sem):
    return pltpu.CompilerParams(dimension_semantics=sem, vmem_limit_bytes=VMEM_LIMIT)


def _adaln_kernel(c_ref, w_ref, b_ref, o_ref):
    c = c_ref[...]
    act = c * jax.nn.sigmoid(c)
    o_ref[...] = jnp.dot(act, w_ref[...], precision=lax.Precision.HIGHEST,
                         preferred_element_type=F32) + b_ref[...]


def _adaln(c8, w_ada, b_ada):
    depth, _, n = w_ada.shape
    bn = 1024
    return pl.pallas_call(
        _adaln_kernel,
        grid=(depth, n // bn),
        in_specs=[
            pl.BlockSpec((SUBLANES, D_MODEL), lambda l, j: (0, 0)),
            pl.BlockSpec((None, D_MODEL, bn), lambda l, j: (l, 0, j)),
            pl.BlockSpec((None, 1, bn), lambda l, j: (l, 0, j)),
        ],
        out_specs=pl.BlockSpec((None, SUBLANES, bn), lambda l, j: (l, 0, j)),
        out_shape=jax.ShapeDtypeStruct((depth, SUBLANES, n), F32),
        compiler_params=_params("arbitrary", "arbitrary"),
        name="adaln_mod",
    )(c8, w_ada, b_ada.reshape(depth, 1, n))


def _cumsum_rows(x):
    n = x.shape[0]
    row = lax.broadcasted_iota(jnp.int32, x.shape, 0)
    s = 1
    while s < n:
        x = x + jnp.where(row >= s, pltpu.roll(x, s, 0), 0.0)
        s *= 2
    return x


def _inproj_kernel(x_ref, scale_ref, shift_ref, w_ref, wvt_ref, wf_ref, bf_ref,
                   pq_ref, pk_ref, cw_ref, cb_ref, wg_ref, bg_ref, lam_ref, gn_ref,
                   yr_ref, za_ref, qx_ref, kx_ref, vt_ref,
                   dcarry_ref, xbuf, zbuf, ybuf, h7_s, p7_s, c_s, hcarry):
    tm = x_ref.shape[0]
    n_slab = D_LRU // LANES

    @pl.when(pl.program_id(1) == 0)
    def _():
        dcarry_ref[...] = jnp.zeros_like(dcarry_ref)
        xbuf[:, 0:SUBLANES, :] = jnp.zeros((n_slab, SUBLANES, LANES), F32)
        hcarry[...] = jnp.zeros_like(hcarry)

    x = x_ref[...]
    mu = jnp.mean(x, axis=-1, keepdims=True)
    xc = x - mu
    var = jnp.mean(xc * xc, axis=-1, keepdims=True)
    h = xc * lax.rsqrt(var + LN_EPS) * (1.0 + scale_ref[...]) + shift_ref[...]
    hb = h.astype(BF16)

    half_w = 2 * LANES

    def proj(lo):
        return jnp.dot(hb, w_ref[:, lo:lo + half_w], preferred_element_type=F32)

    def to_slabs(dst, row0, val, half):
        for s2 in range(2):
            dst[2 * half + s2, row0:row0 + tm, :] = val[:, s2 * LANES:(s2 + 1) * LANES]

    for half in range(2):
        to_slabs(xbuf, SUBLANES, proj(half * half_w), half)
    conv = _lru_conv(xbuf, cw_ref, cb_ref, tm)
    for half in range(2):
        to_slabs(zbuf, 0, proj(D_LRU + half * half_w), half)
    g = jnp.dot(conv.astype(BF16), wg_ref[...], preferred_element_type=F32) + bg_ref[...]
    nl = -lam_ref[...]
    sp8 = LRU_C * (jnp.maximum(nl, 0.0) + jnp.log1p(jnp.exp(-jnp.abs(nl))))

    def piece(n):
        kind, half = divmod(n, 2)
        if kind == 0:
            za_ref[:, half * half_w:(half + 1) * half_w] = (
                proj(2 * D_LRU + 3 * D_ATTN + half * half_w).astype(za_ref.dtype))
        elif kind in (1, 2):
            dst, lo, mul = ((qx_ref, 2 * D_LRU, HEAD_DIM ** -0.5 * LOG2E) if kind == 1
                            else (kx_ref, 2 * D_LRU + D_ATTN, None))
            val = proj(lo + half * half_w)
            if mul is not None:
                val = val * mul
            for s2 in range(2):
                p = 2 * half + s2
                dst[:, 2 * p * LANES:(2 * p + 1) * LANES] = (
                    val[:, s2 * LANES:(s2 + 1) * LANES].astype(BF16))
        else:
            vt = lax.dot_general(wvt_ref[half * half_w:(half + 1) * half_w, :], hb,
                                 (((1,), (1,)), ((), ())), preferred_element_type=F32)
            ones = jnp.ones((BF16_ROWS, tm), BF16)
            for h4 in range(N_HEADS // 2):
                hd = half * (N_HEADS // 2) + h4
                vt_ref[hd * V_ROWS:hd * V_ROWS + HEAD_DIM, :] = (
                    vt[h4 * HEAD_DIM:(h4 + 1) * HEAD_DIM, :].astype(BF16))
                vt_ref[hd * V_ROWS + HEAD_DIM:(hd + 1) * V_ROWS, :] = ones

    ng = tm // SUBLANES
    hs, ps = [], []
    for j in range(SUBLANES):
        rows = slice(j * ng, (j + 1) * ng)
        aj, uj = _lru_gates(g[rows, :], conv[rows, :], sp8)
        if j == 0:
            hl, pr = uj, aj
        else:
            hl = aj * hl + uj
            pr = aj * pr
        hs.append(hl)
        ps.append(pr)
        piece(j)

    fl = jnp.dot(hb, wf_ref[...], preferred_element_type=F32) + bf_ref[...]
    logf = jnp.minimum(fl, 0.0) - jnp.log1p(jnp.exp(-jnp.abs(fl)))
    d = _cumsum_rows(logf) + dcarry_ref[0:1, :]
    dcarry_ref[...] = jnp.broadcast_to(d[tm - 1:tm, :], dcarry_ref.shape)
    d2 = d * LOG2E
    hi = d2.astype(BF16).astype(F32)
    r1 = d2 - hi
    mid = r1.astype(BF16).astype(F32)
    lo = (r1 - mid).astype(BF16).astype(F32)
    lane = lax.broadcasted_iota(jnp.int32, d2.shape, 1)
    dsplit = jnp.where(lane < N_HEADS, hi,
             jnp.where(lane < 2 * N_HEADS, mid,
             jnp.where(lane < 3 * N_HEADS, lo,
             jnp.where(lane == 3 * N_HEADS, 1.0, 0.0)))).astype(BF16)
    qbias = jnp.dot(dsplit, pq_ref[...], preferred_element_type=F32)
    kbias = jnp.dot(dsplit, pk_ref[...], preferred_element_type=F32)
    for p in range(N_PAIRS):
        src = slice(p * LANES, (p + 1) * LANES)
        qx_ref[:, (2 * p + 1) * LANES:(2 * p + 2) * LANES] = qbias[:, src].astype(BF16)
        kx_ref[:, (2 * p + 1) * LANES:(2 * p + 2) * LANES] = kbias[:, src].astype(BF16)

    _lru_scan_norm(hs, ps, zbuf, ybuf, gn_ref, h7_s, p7_s, c_s, hcarry, tm)
    for sl in range(n_slab):
        yr_ref[:, sl * LANES:(sl + 1) * LANES] = ybuf[sl].astype(yr_ref.dtype)


def _inproj(x, scale, shift, w_main, wvt, wf, bf, pq, pk, cw, cb, wg, bg, lam, gn):
    b, s, _ = x.shape
    tm = TOKEN_TILE
    nt = s // tm
    ng = tm // SUBLANES
    n_slab = D_LRU // LANES
    tok = lambda w: pl.BlockSpec((None, tm, w), lambda i, t: (i, t, 0))
    full = lambda a: pl.BlockSpec(a.shape, lambda i, t: (0,) * a.ndim)
    vec = pl.BlockSpec((None, 1, D_MODEL), lambda i, t: (i, 0, 0))
    consts = (w_main, wvt, wf, bf, pq, pk, cw, cb, wg, bg, lam, gn)
    return pl.pallas_call(
        _inproj_kernel,
        grid=(b, nt),
        in_specs=[tok(D_MODEL), vec, vec] + [full(a) for a in consts],
        out_specs=[tok(D_LRU), tok(D_ATTN), tok(2 * D_ATTN), tok(2 * D_ATTN),
                   pl.BlockSpec((None, None, N_HEADS * V_ROWS, tm), lambda i, t: (i, t, 0, 0))],
        out_shape=[jax.ShapeDtypeStruct((b, s, D_LRU), BF16),
                   jax.ShapeDtypeStruct((b, s, D_ATTN), BF16),
                   jax.ShapeDtypeStruct((b, s, 2 * D_ATTN), BF16),
                   jax.ShapeDtypeStruct((b, s, 2 * D_ATTN), BF16),
                   jax.ShapeDtypeStruct((b, nt, N_HEADS * V_ROWS, tm), BF16)],
        scratch_shapes=[pltpu.VMEM((SUBLANES, LANES), F32),
                        pltpu.VMEM((n_slab, tm + SUBLANES, LANES), F32),
                        pltpu.VMEM((n_slab, tm, LANES), F32),
                        pltpu.VMEM((n_slab, tm, LANES), F32),
                        pltpu.VMEM((ng, D_LRU), F32),
                        pltpu.VMEM((ng, D_LRU), F32),
                        pltpu.VMEM((ng, D_LRU), F32),
                        pltpu.VMEM((SUBLANES, D_LRU), F32)],
        compiler_params=_params("arbitrary", "arbitrary"),
        name="ln_inproj_rglru",
    )(x, scale, shift, *consts)


def _lru_conv(xbuf, cw_ref, cb_ref, tc):
    ng = tc // SUBLANES
    n_slab = D_LRU // LANES
    lanes = lambda sl: slice(sl * LANES, (sl + 1) * LANES)
    grp = lambda first: pl.ds(first, ng, stride=SUBLANES)
    blocks = []
    for j in range(SUBLANES):
        cols = []
        for sl in range(n_slab):
            acc = cb_ref[:, lanes(sl)]
            for kk in range(CONV_W):
                first = SUBLANES + j - (CONV_W - 1) + kk
                acc = acc + cw_ref[kk:kk + 1, lanes(sl)] * xbuf[sl, grp(first), :]
            cols.append(acc)
        blocks.append(jnp.concatenate(cols, axis=1))
    for sl in range(n_slab):
        xbuf[sl, 0:SUBLANES, :] = xbuf[sl, tc:tc + SUBLANES, :]
    return jnp.concatenate(blocks, axis=0)


def _lru_gates(g, xc, sp8):
    r = jax.nn.sigmoid(g[:, :D_LRU])
    i = jax.nn.sigmoid(g[:, D_LRU:])
    w = r * sp8
    a = jnp.exp(-w)
    m2 = jnp.tanh(w) * (1.0 + a * a)
    mult = m2 * lax.rsqrt(jnp.maximum(m2, TINY))
    return a, mult * (i * xc)


def _lru_scan_norm(hs, ps, zbuf, ybuf, g_ref, h7_s, p7_s, c_s, hcarry, tc):
    ng = tc // SUBLANES
    n_slab = D_LRU // LANES
    lanes = lambda sl: slice(sl * LANES, (sl + 1) * LANES)
    grp = lambda first: pl.ds(first, ng, stride=SUBLANES)
    h7_s[...] = hs[-1]
    p7_s[...] = ps[-1]

    c = hcarry[0:1, :]
    for gi in range(ng):
        c_s[gi:gi + 1, :] = c
        c = p7_s[gi:gi + 1, :] * c + h7_s[gi:gi + 1, :]
    hcarry[0:1, :] = c
    cin = c_s[...]

    for j in range(SUBLANES):
        hj = hs[j] + ps[j] * cin
        ms = jnp.mean(hj * hj, axis=-1, keepdims=True)
        z = jnp.concatenate([zbuf[sl, grp(j), :] for sl in range(n_slab)], axis=1)
        y = hj * lax.rsqrt(ms + LN_EPS) * g_ref[...] * (z * jax.nn.sigmoid(z))
        for sl in range(n_slab):
            ybuf[sl, grp(j), :] = y[:, lanes(sl)]


_NT = (((1,), (1,)), ((), ()))


def _attn_kernel(q_ref, k_ref, v_ref, o_ref, q_s, q_n, s_a, s_b, mx_a, mx_b, acc_ref, m_ref):
    tq = o_ref.shape[0]
    tk = tq // 2
    qi = pl.program_id(2)
    nq = pl.num_programs(2)

    def load_q(t, dst):
        qf = q_ref[pl.ds(pl.multiple_of(t * tq, tq), tq), :].astype(F32)
        lane = lax.broadcasted_iota(jnp.int32, qf.shape, 1)
        nb = 6
        in_a = (lane < HEAD_DIM) | ((lane >= 2 * HEAD_DIM) & (lane < 2 * HEAD_DIM + nb))
        in_b = (((lane >= HEAD_DIM) & (lane < 2 * HEAD_DIM))
                | ((lane >= 2 * HEAD_DIM + nb) & (lane < 2 * HEAD_DIM + 2 * nb)))
        dst[0] = jnp.where(in_a, qf, 0.0).astype(BF16)
        dst[1] = jnp.where(in_b, qf, 0.0).astype(BF16)

    bufs = ((s_a, mx_a), (s_b, mx_b))

    def first_col(half, diag):
        return half * tk if diag else 0

    def scores(kv, half, diag, q_src=q_s):
        s_ref, mx_ref = bufs[half]
        c0 = first_col(half, diag)
        start = pl.multiple_of(kv * tq + half * tk, tk)
        k = k_ref[pl.ds(start, tk), :]
        for hd in range(2):
            s = lax.dot_general(k, q_src[hd, c0:, :], _NT, preferred_element_type=F32)
            if diag:
                kpos = half * tk + lax.broadcasted_iota(jnp.int32, s.shape, 0)
                qpos = c0 + lax.broadcasted_iota(jnp.int32, s.shape, 1)
                s = jnp.where(kpos <= qpos, s, MASK_VALUE)
            s_ref[hd, :, c0:] = s
            mx_ref[hd, :, c0:] = jnp.max(s, axis=0, keepdims=True)

    def consume(kv, half, diag=False):
        s_ref, mx_ref = bufs[half]
        c0 = first_col(half, diag)
        for hd in range(2):
            m_old = m_ref[hd, :, c0:]
            m_new = jnp.maximum(m_old, mx_ref[hd, :, c0:])
            alpha = jnp.exp2(m_old - m_new)
            p = jnp.exp2(s_ref[hd, :, c0:] - m_new).astype(BF16)
            rows = slice(hd * V_ROWS, (hd + 1) * V_ROWS)
            if v_ref.shape[-1] == tk:
                vt = v_ref[2 * kv + half, rows, :]
            else:
                vt = v_ref[kv, rows, half * tk:(half + 1) * tk]
            acc_ref[hd, :, c0:] = (alpha * acc_ref[hd, :, c0:]
                                   + jnp.dot(vt, p, preferred_element_type=F32))
            m_ref[hd, :, c0:] = m_new

    m_ref[...] = jnp.full(m_ref.shape, MASK_VALUE, F32)
    acc_ref[...] = jnp.zeros_like(acc_ref)

    load_q(qi, q_s)

    @pl.when(qi == 0)
    def _():
        scores(0, 0, True)

    def full_step(kv, carry):
        scores(kv, 1, False)
        consume(kv, 0)
        scores(kv + 1, 0, False)
        consume(kv, 1)
        return carry

    lax.fori_loop(0, qi - 1, full_step, 0)

    @pl.when(qi > 0)
    def _():
        scores(qi - 1, 1, False)
        consume(qi - 1, 0)
        scores(qi, 0, True)
        consume(qi - 1, 1)

    def drain(prefetch_next):
        scores(qi, 1, True)
        consume(qi, 0)
        if prefetch_next:
            load_q(qi + 1, q_n)
            scores(0, 0, False, q_n)
        consume(qi, 1, True)

    @pl.when(qi + 1 < nq)
    def _():
        drain(True)

    @pl.when(qi + 1 == nq)
    def _():
        drain(False)

    outs = []
    for hd in range(2):
        acc = acc_ref[hd]
        outs.append(acc[:HEAD_DIM, :] / acc[HEAD_DIM:HEAD_DIM + 1, :])
    o_ref[...] = jnp.concatenate(outs, axis=0).T


def _attention(qx, kx, vt):
    b, s, _ = qx.shape
    nkv, _, tk = vt.shape[1:]
    tq = Q_TILE
    assert tq // 2 in (tk, tk // 2)
    return pl.pallas_call(
        _attn_kernel,
        grid=(b, N_PAIRS, s // tq),
        in_specs=[
            pl.BlockSpec((None, s, 2 * LANES), lambda i, p, t: (i, 0, p)),
            pl.BlockSpec((None, s, 2 * LANES), lambda i, p, t: (i, 0, p)),
            pl.BlockSpec((None, nkv, 2 * V_ROWS, tk), lambda i, p, t: (i, 0, p, 0)),
        ],
        out_specs=pl.BlockSpec((None, tq, LANES), lambda i, p, t: (i, t, p)),
        out_shape=jax.ShapeDtypeStruct((b, s, D_ATTN), F32),
        scratch_shapes=[pltpu.VMEM((2, tq, 2 * LANES), BF16),
                        pltpu.VMEM((2, tq, 2 * LANES), BF16),
                        pltpu.VMEM((2, tq // 2, tq), F32),
                        pltpu.VMEM((2, tq // 2, tq), F32),
                        pltpu.VMEM((2, 1, tq), F32),
                        pltpu.VMEM((2, 1, tq), F32),
                        pltpu.VMEM((2, V_ROWS, tq), F32),
                        pltpu.VMEM((2, 1, tq), F32)],
        compiler_params=_params("arbitrary", "arbitrary", "arbitrary"),
        name="fox_attention",
    )(qx, kx, vt)


def _outproj_kernel(alpha, yr_ref, ya_ref, za_ref, x_ref, gate_ref, na_ref, w_ref,
                    g_ref, b_ref, o_ref):
    ya = ya_ref[...]
    ms = jnp.mean(ya * ya, axis=-1, keepdims=True)
    za = za_ref[...].astype(F32)
    ya = ya * lax.rsqrt(ms + LN_EPS) * na_ref[...] * (za * jax.nn.sigmoid(za))
    y = (jnp.dot(yr_ref[...], w_ref[0:D_LRU, :], preferred_element_type=F32)
         + jnp.dot(ya.astype(BF16), w_ref[D_LRU:, :], preferred_element_type=F32))
    res = alpha * x_ref[...] + gate_ref[...] * y
    mu = jnp.mean(res, axis=-1, keepdims=True)
    rc = res - mu
    var = jnp.mean(rc * rc, axis=-1, keepdims=True)
    o_ref[...] = rc * lax.rsqrt(var + LN_EPS) * g_ref[...] + b_ref[...]


def _outproj(alpha, yr, ya, za, x, gate, na, w, g, bb):
    b, s, _ = x.shape
    tm = TOKEN_TILE
    tok = lambda w_: pl.BlockSpec((None, tm, w_), lambda i, t: (i, t, 0))
    full = lambda a: pl.BlockSpec(a.shape, lambda i, t: (0,) * a.ndim)
    vec = pl.BlockSpec((None, 1, D_MODEL), lambda i, t: (i, 0, 0))
    return pl.pallas_call(
        functools.partial(_outproj_kernel, alpha),
        grid=(b, s // tm),
        in_specs=[tok(D_LRU), tok(D_ATTN), tok(D_ATTN), tok(D_MODEL), vec, full(na),
                  full(w), full(g), full(bb)],
        out_specs=tok(D_MODEL),
        out_shape=jax.ShapeDtypeStruct((b, s, D_MODEL), F32),
        compiler_params=_params("arbitrary", "arbitrary"),
        name="outproj_ln",
    )(yr, ya, za, x, gate, na, w, g, bb)


def _bias_placement():
    pq = np.zeros((LANES, D_ATTN), np.float32)
    pk = np.zeros((LANES, D_ATTN), np.float32)
    one = 3 * N_HEADS
    for hd in range(N_HEADS):
        base = (hd // 2) * LANES + (hd % 2) * 6
        for part in range(3):
            src = part * N_HEADS + hd
            pq[src, base + part] = 1.0
            pq[one, base + 3 + part] = 1.0
            pk[one, base + part] = 1.0
            pk[src, base + 3 + part] = -1.0
    return jnp.asarray(pq, BF16), jnp.asarray(pk, BF16)


def _block_diag(w):
    nb, bw, _ = w.shape
    eye = jnp.eye(nb, dtype=w.dtype)
    return (eye[:, None, :, None] * w[:, :, None, :]).reshape(nb * bw, nb * bw)


def kernel(x, c, w_ada, b_ada, w_in, b_fgate, conv_w, conv_b, w_gate_a, b_gate_a,
           w_gate_x, b_gate_x, lru_lambda, norm_lru, norm_attn, w_out, ln_gain, ln_bias):
    depth = w_ada.shape[0]
    batch = x.shape[0]
    alpha = (2.0 * depth) ** 0.25
    n_main = 2 * D_LRU + 4 * D_ATTN

    c8 = jnp.pad(c, ((0, SUBLANES - batch), (0, 0)))
    mod = _adaln(c8, w_ada, b_ada)[:, :batch, :]
    pq, pk = _bias_placement()

    for l in range(depth):
        shift = mod[l, :, 0:D_MODEL].reshape(batch, 1, D_MODEL)
        scale = mod[l, :, D_MODEL:2 * D_MODEL].reshape(batch, 1, D_MODEL)
        gate = mod[l, :, 2 * D_MODEL:].reshape(batch, 1, D_MODEL)

        w_main = w_in[l, :, :n_main].astype(BF16)
        wvt = w_in[l, :, 2 * D_LRU + 2 * D_ATTN:2 * D_LRU + 3 * D_ATTN].T.astype(BF16)
        wf = jnp.zeros((D_MODEL, LANES), F32).at[:, :3 * N_HEADS].set(
            jnp.tile(w_in[l, :, n_main:], (1, 3))).astype(BF16)
        bf = jnp.zeros((1, LANES), F32).at[0, :3 * N_HEADS].set(jnp.tile(b_fgate[l], 3))
        wg = jnp.concatenate([_block_diag(w_gate_a[l]), _block_diag(w_gate_x[l])],
                             axis=1).astype(BF16)
        bg = jnp.concatenate([b_gate_a[l], b_gate_x[l]]).reshape(1, 2 * D_LRU)
        yr, za, qx, kx, vt = _inproj(
            x, scale, shift, w_main, wvt, wf, bf, pq, pk, conv_w[l],
            conv_b[l].reshape(1, D_LRU), wg, bg, lru_lambda[l].reshape(1, D_LRU),
            norm_lru[l].reshape(1, D_LRU))

        ya = _attention(qx, kx, vt)

        x = _outproj(alpha, yr, ya, za, x, gate, norm_attn[l].reshape(1, D_ATTN),
                     w_out[l].astype(BF16), ln_gain[l].reshape(1, D_MODEL),
                     ln_bias[l].reshape(1, D_MODEL))
    return x
```

```python
import functools

import numpy as np
import jax
import jax.numpy as jnp
from jax import lax
from jax.experimental import pallas as pl
from jax.experimental.pallas import tpu as pltpu

F32 = jnp.float32
BF16 = jnp.bfloat16

D_MODEL = 1024
D_LRU = 512
D_ATTN = 512
N_HEADS = 8
HEAD_DIM = 64
N_PAIRS = N_HEADS // 2
LRU_BLOCKS = 8
CONV_W = 4
LRU_C = 8.0
LN_EPS = 1e-5
MASK_VALUE = -1e30
TINY = 1e-37
LOG2E = 1.4426950408889634

SUBLANES = 8
LANES = 128
BF16_ROWS = 16
V_ROWS = HEAD_DIM + BF16_ROWS

TOKEN_TILE = 512
OUT_TILE = 1024
Q_TILE = 1024
ATTN_STRIP = 256
KV_TILE = TOKEN_TILE
VMEM_LIMIT = 56 * 1024 * 1024


def _params(*sem):
    return pltpu.CompilerParams(dimension_semantics=sem, vmem_limit_bytes=VMEM_LIMIT)


def _adaln_kernel(c_ref, w_ref, b_ref, o_ref):
    c = c_ref[...]
    act = c * jax.nn.sigmoid(c)
    o_ref[...] = jnp.dot(act, w_ref[...], precision=lax.Precision.HIGHEST,
                         preferred_element_type=F32) + b_ref[...]


def _adaln(c8, w_ada, b_ada):
    depth, _, n = w_ada.shape
    bn = 1024
    return pl.pallas_call(
        _adaln_kernel,
        grid=(depth, n // bn),
        in_specs=[
            pl.BlockSpec((SUBLANES, D_MODEL), lambda l, j: (0, 0)),
            pl.BlockSpec((None, D_MODEL, bn), lambda l, j: (l, 0, j)),
            pl.BlockSpec((None, 1, bn), lambda l, j: (l, 0, j)),
        ],
        out_specs=pl.BlockSpec((None, SUBLANES, bn), lambda l, j: (l, 0, j)),
        out_shape=jax.ShapeDtypeStruct((depth, SUBLANES, n), F32),
        compiler_params=_params("arbitrary", "arbitrary"),
        name="adaln_mod",
    )(c8, w_ada, b_ada.reshape(depth, 1, n))


def _cumsum_rows(x):
    n = x.shape[0]
    row = lax.broadcasted_iota(jnp.int32, x.shape, 0)
    s = 1
    while s < n:
        x = x + jnp.where(row >= s, pltpu.roll(x, s, 0), 0.0)
        s *= 2
    return x


def _inproj_kernel(x_ref, scale_ref, shift_ref, w_ref, wvt_ref, wf_ref, bf_ref,
                   pq_ref, pk_ref, cw_ref, cb_ref, wg_ref, bg_ref, lam_ref, gn_ref,
                   yr_ref, za_ref, qx_ref, kx_ref, vt_ref,
                   dcarry_ref, xbuf, zbuf, ybuf, h7_s, p7_s, c_s, hcarry):
    tm = x_ref.shape[0]
    n_slab = D_LRU // LANES

    @pl.when(pl.program_id(1) == 0)
    def _():
        dcarry_ref[...] = jnp.zeros_like(dcarry_ref)
        xbuf[:, 0:SUBLANES, :] = jnp.zeros((n_slab, SUBLANES, LANES), F32)
        hcarry[...] = jnp.zeros_like(hcarry)

    x = x_ref[...]
    mu = jnp.mean(x, axis=-1, keepdims=True)
    xc = x - mu
    var = jnp.mean(xc * xc, axis=-1, keepdims=True)
    h = xc * lax.rsqrt(var + LN_EPS) * (1.0 + scale_ref[...]) + shift_ref[...]
    hb = h.astype(BF16)

    half_w = 2 * LANES

    def proj(lo):
        return jnp.dot(hb, w_ref[:, lo:lo + half_w], preferred_element_type=F32)

    def to_slabs(dst, row0, val, half):
        for s2 in range(2):
            dst[2 * half + s2, row0:row0 + tm, :] = val[:, s2 * LANES:(s2 + 1) * LANES]

    for half in range(2):
        to_slabs(xbuf, SUBLANES, proj(half * half_w), half)
    conv = _lru_conv(xbuf, cw_ref, cb_ref, tm)
    for half in range(2):
        to_slabs(zbuf, 0, proj(D_LRU + half * half_w), half)
    g = jnp.dot(conv.astype(BF16), wg_ref[...], preferred_element_type=F32) + bg_ref[...]
    nl = -lam_ref[...]
    sp8 = LRU_C * (jnp.maximum(nl, 0.0) + jnp.log1p(jnp.exp(-jnp.abs(nl))))

    def piece(n):
        kind, half = divmod(n, 2)
        if kind == 0:
            za_ref[:, half * half_w:(half + 1) * half_w] = (
                proj(2 * D_LRU + 3 * D_ATTN + half * half_w).astype(za_ref.dtype))
        elif kind in (1, 2):
            dst, lo, mul = ((qx_ref, 2 * D_LRU, HEAD_DIM ** -0.5 * LOG2E) if kind == 1
                            else (kx_ref, 2 * D_LRU + D_ATTN, None))
            val = proj(lo + half * half_w)
            if mul is not None:
                val = val * mul
            for s2 in range(2):
                p = 2 * half + s2
                dst[:, 2 * p * LANES:(2 * p + 1) * LANES] = (
                    val[:, s2 * LANES:(s2 + 1) * LANES].astype(BF16))
        else:
            vt = lax.dot_general(wvt_ref[half * half_w:(half + 1) * half_w, :], hb,
                                 (((1,), (1,)), ((), ())), preferred_element_type=F32)
            ones = jnp.ones((BF16_ROWS, tm), BF16)
            for h4 in range(N_HEADS // 2):
                hd = half * (N_HEADS // 2) + h4
                vt_ref[hd * V_ROWS:hd * V_ROWS + HEAD_DIM, :] = (
                    vt[h4 * HEAD_DIM:(h4 + 1) * HEAD_DIM, :].astype(BF16))
                vt_ref[hd * V_ROWS + HEAD_DIM:(hd + 1) * V_ROWS, :] = ones

    ng = tm // SUBLANES
    hs, ps = [], []
    for j in range(SUBLANES):
        rows = slice(j * ng, (j + 1) * ng)
        aj, uj = _lru_gates(g[rows, :], conv[rows, :], sp8)
        if j == 0:
            hl, pr = uj, aj
        else:
            hl = aj * hl + uj
            pr = aj * pr
        hs.append(hl)
        ps.append(pr)
        piece(j)

    fl = jnp.dot(hb, wf_ref[...], preferred_element_type=F32) + bf_ref[...]
    logf = jnp.minimum(fl, 0.0) - jnp.log1p(jnp.exp(-jnp.abs(fl)))
    d = _cumsum_rows(logf) + dcarry_ref[0:1, :]
    dcarry_ref[...] = jnp.broadcast_to(d[tm - 1:tm, :], dcarry_ref.shape)
    d2 = d * LOG2E
    hi = d2.astype(BF16).astype(F32)
    r1 = d2 - hi
    mid = r1.astype(BF16).astype(F32)
    lo = (r1 - mid).astype(BF16).astype(F32)
    lane = lax.broadcasted_iota(jnp.int32, d2.shape, 1)
    dsplit = jnp.where(lane < N_HEADS, hi,
             jnp.where(lane < 2 * N_HEADS, mid,
             jnp.where(lane < 3 * N_HEADS, lo,
             jnp.where(lane == 3 * N_HEADS, 1.0, 0.0)))).astype(BF16)
    qbias = jnp.dot(dsplit, pq_ref[...], preferred_element_type=F32)
    kbias = jnp.dot(dsplit, pk_ref[...], preferred_element_type=F32)
    for p in range(N_PAIRS):
        src = slice(p * LANES, (p + 1) * LANES)
        qx_ref[:, (2 * p + 1) * LANES:(2 * p + 2) * LANES] = qbias[:, src].astype(BF16)
        kx_ref[:, (2 * p + 1) * LANES:(2 * p + 2) * LANES] = kbias[:, src].astype(BF16)

    _lru_scan_norm(hs, ps, zbuf, ybuf, gn_ref, h7_s, p7_s, c_s, hcarry, tm)
    for sl in range(n_slab):
        yr_ref[:, sl * LANES:(sl + 1) * LANES] = ybuf[sl].astype(yr_ref.dtype)


def _inproj(x, scale, shift, w_main, wvt, wf, bf, pq, pk, cw, cb, wg, bg, lam, gn):
    b, s, _ = x.shape
    tm = TOKEN_TILE
    nt = s // tm
    ng = tm // SUBLANES
    n_slab = D_LRU // LANES
    tok = lambda w: pl.BlockSpec((None, tm, w), lambda i, t: (i, t, 0))
    full = lambda a: pl.BlockSpec(a.shape, lambda i, t: (0,) * a.ndim)
    vec = pl.BlockSpec((None, 1, D_MODEL), lambda i, t: (i, 0, 0))
    consts = (w_main, wvt, wf, bf, pq, pk, cw, cb, wg, bg, lam, gn)
    return pl.pallas_call(
        _inproj_kernel,
        grid=(b, nt),
        in_specs=[tok(D_MODEL), vec, vec] + [full(a) for a in consts],
        out_specs=[tok(D_LRU), tok(D_ATTN), tok(2 * D_ATTN), tok(2 * D_ATTN),
                   pl.BlockSpec((None, None, N_HEADS * V_ROWS, tm), lambda i, t: (i, t, 0, 0))],
        out_shape=[jax.ShapeDtypeStruct((b, s, D_LRU), BF16),
                   jax.ShapeDtypeStruct((b, s, D_ATTN), BF16),
                   jax.ShapeDtypeStruct((b, s, 2 * D_ATTN), BF16),
                   jax.ShapeDtypeStruct((b, s, 2 * D_ATTN), BF16),
                   jax.ShapeDtypeStruct((b, nt, N_HEADS * V_ROWS, tm), BF16)],
        scratch_shapes=[pltpu.VMEM((SUBLANES, LANES), F32),
                        pltpu.VMEM((n_slab, tm + SUBLANES, LANES), F32),
                        pltpu.VMEM((n_slab, tm, LANES), F32),
                        pltpu.VMEM((n_slab, tm, LANES), F32),
                        pltpu.VMEM((ng, D_LRU), F32),
                        pltpu.VMEM((ng, D_LRU), F32),
                        pltpu.VMEM((ng, D_LRU), F32),
                        pltpu.VMEM((SUBLANES, D_LRU), F32)],
        compiler_params=_params("arbitrary", "arbitrary"),
        name="ln_inproj_rglru",
    )(x, scale, shift, *consts)


def _lru_conv(xbuf, cw_ref, cb_ref, tc):
    ng = tc // SUBLANES
    n_slab = D_LRU // LANES
    lanes = lambda sl: slice(sl * LANES, (sl + 1) * LANES)
    grp = lambda first: pl.ds(first, ng, stride=SUBLANES)
    blocks = []
    for j in range(SUBLANES):
        cols = []
        for sl in range(n_slab):
            acc = cb_ref[:, lanes(sl)]
            for kk in range(CONV_W):
                first = SUBLANES + j - (CONV_W - 1) + kk
                acc = acc + cw_ref[kk:kk + 1, lanes(sl)] * xbuf[sl, grp(first), :]
            cols.append(acc)
        blocks.append(jnp.concatenate(cols, axis=1))
    for sl in range(n_slab):
        xbuf[sl, 0:SUBLANES, :] = xbuf[sl, tc:tc + SUBLANES, :]
    return jnp.concatenate(blocks, axis=0)


def _lru_gates(g, xc, sp8):
    r = jax.nn.sigmoid(g[:, :D_LRU])
    i = jax.nn.sigmoid(g[:, D_LRU:])
    w = r * sp8
    a = jnp.exp(-w)
    m2 = jnp.tanh(w) * (1.0 + a * a)
    mult = m2 * lax.rsqrt(jnp.maximum(m2, TINY))
    return a, mult * (i * xc)


def _lru_scan_norm(hs, ps, zbuf, ybuf, g_ref, h7_s, p7_s, c_s, hcarry, tc):
    ng = tc // SUBLANES
    n_slab = D_LRU // LANES
    lanes = lambda sl: slice(sl * LANES, (sl + 1) * LANES)
    grp = lambda first: pl.ds(first, ng, stride=SUBLANES)
    h7_s[...] = hs[-1]
    p7_s[...] = ps[-1]

    c = hcarry[0:1, :]
    for gi in range(ng):
        c_s[gi:gi + 1, :] = c
        c = p7_s[gi:gi + 1, :] * c + h7_s[gi:gi + 1, :]
    hcarry[0:1, :] = c
    cin = c_s[...]

    for j in range(SUBLANES):
        hj = hs[j] + ps[j] * cin
        ms = jnp.mean(hj * hj, axis=-1, keepdims=True)
        z = jnp.concatenate([zbuf[sl, grp(j), :] for sl in range(n_slab)], axis=1)
        y = hj * lax.rsqrt(ms + LN_EPS) * g_ref[...] * (z * jax.nn.sigmoid(z))
        for sl in range(n_slab):
            ybuf[sl, grp(j), :] = y[:, lanes(sl)]


_NT = (((1,), (1,)), ((), ()))


def _attn_kernel(q_ref, k_ref, v_ref, o_ref, q_s, q_n, s_a, s_b, mx_a, mx_b, acc_ref, m_ref):
    tq = o_ref.shape[0]
    tk = tq // 2
    qi = pl.program_id(2)
    nq = pl.num_programs(2)

    def load_q(t, dst):
        qf = q_ref[pl.ds(pl.multiple_of(t * tq, tq), tq), :].astype(F32)
        lane = lax.broadcasted_iota(jnp.int32, qf.shape, 1)
        nb = 6
        in_a = (lane < HEAD_DIM) | ((lane >= 2 * HEAD_DIM) & (lane < 2 * HEAD_DIM + nb))
        in_b = (((lane >= HEAD_DIM) & (lane < 2 * HEAD_DIM))
                | ((lane >= 2 * HEAD_DIM + nb) & (lane < 2 * HEAD_DIM + 2 * nb)))
        dst[0] = jnp.where(in_a, qf, 0.0).astype(BF16)
        dst[1] = jnp.where(in_b, qf, 0.0).astype(BF16)

    bufs = ((s_a, mx_a), (s_b, mx_b))

    def skipped(st, half, diag):
        return diag and st * ATTN_STRIP < half * tk

    def score_strip(st, hd, kv, half, diag, q_src=q_s):
        s_ref, mx_ref = bufs[half]
        cols = slice(st * ATTN_STRIP, (st + 1) * ATTN_STRIP)
        start = pl.multiple_of(kv * tq + half * tk, tk)
        k = k_ref[pl.ds(start, tk), :]
        s = lax.dot_general(k, q_src[hd, cols, :], _NT, preferred_element_type=F32)
        if diag and st * ATTN_STRIP < (half + 1) * tk - 1:
            kpos = half * tk + lax.broadcasted_iota(jnp.int32, s.shape, 0)
            qpos = st * ATTN_STRIP + lax.broadcasted_iota(jnp.int32, s.shape, 1)
            s = jnp.where(kpos <= qpos, s, MASK_VALUE)
        s_ref[hd, :, cols] = s
        mx_ref[hd, :, cols] = jnp.max(s, axis=0, keepdims=True)

    def consume_strip(st, hd, kv, half, diag=False):
        s_ref, mx_ref = bufs[half]
        cols = slice(st * ATTN_STRIP, (st + 1) * ATTN_STRIP)
        m_old = m_ref[hd, :, cols]
        m_new = jnp.maximum(m_old, mx_ref[hd, :, cols])
        alpha = jnp.exp2(m_old - m_new)
        p = jnp.exp2(s_ref[hd, :, cols] - m_new).astype(BF16)
        rows = slice(hd * V_ROWS, (hd + 1) * V_ROWS)
        if v_ref.shape[-1] == tk:
            vt = v_ref[2 * kv + half, rows, :]
        else:
            vt = v_ref[kv, rows, half * tk:(half + 1) * tk]
        acc_ref[hd, :, cols] = (alpha * acc_ref[hd, :, cols]
                                + jnp.dot(vt, p, preferred_element_type=F32))
        m_ref[hd, :, cols] = m_new

    def step(score_args, consume_args):
        for st in range(tq // ATTN_STRIP):
            for hd in range(2):
                if score_args is not None and not skipped(st, *score_args[1:3]):
                    score_strip(st, hd, *score_args)
                if consume_args is not None and not skipped(st, *consume_args[1:3]):
                    consume_strip(st, hd, *consume_args)

    m_ref[...] = jnp.full(m_ref.shape, MASK_VALUE, F32)
    acc_ref[...] = jnp.zeros_like(acc_ref)

    load_q(qi, q_s)

    @pl.when(qi == 0)
    def _():
        step((0, 0, True), None)

    def full_step(kv, carry):
        step((kv, 1, False), (kv, 0, False))
        step((kv + 1, 0, False), (kv, 1, False))
        return carry

    lax.fori_loop(0, qi - 1, full_step, 0)

    @pl.when(qi > 0)
    def _():
        step((qi - 1, 1, False), (qi - 1, 0, False))
        step((qi, 0, True), (qi - 1, 1, False))

    def drain(prefetch_next):
        step((qi, 1, True), (qi, 0, False))
        if prefetch_next:
            load_q(qi + 1, q_n)
            step((0, 0, False, q_n), (qi, 1, True))
        else:
            step(None, (qi, 1, True))

    @pl.when(qi + 1 < nq)
    def _():
        drain(True)

    @pl.when(qi + 1 == nq)
    def _():
        drain(False)

    outs = []
    for hd in range(2):
        acc = acc_ref[hd]
        outs.append(acc[:HEAD_DIM, :] / acc[HEAD_DIM:HEAD_DIM + 1, :])
    o_ref[...] = jnp.concatenate(outs, axis=0).T


def _attention(qx, kx, vt):
    b, s, _ = qx.shape
    nkv, _, tk = vt.shape[1:]
    tq = Q_TILE
    assert tq // 2 in (tk, tk // 2)
    return pl.pallas_call(
        _attn_kernel,
        grid=(b, N_PAIRS, s // tq),
        in_specs=[
            pl.BlockSpec((None, s, 2 * LANES), lambda i, p, t: (i, 0, p)),
            pl.BlockSpec((None, s, 2 * LANES), lambda i, p, t: (i, 0, p)),
            pl.BlockSpec((None, nkv, 2 * V_ROWS, tk), lambda i, p, t: (i, 0, p, 0)),
        ],
        out_specs=pl.BlockSpec((None, tq, LANES), lambda i, p, t: (i, t, p)),
        out_shape=jax.ShapeDtypeStruct((b, s, D_ATTN), F32),
        scratch_shapes=[pltpu.VMEM((2, tq, 2 * LANES), BF16),
                        pltpu.VMEM((2, tq, 2 * LANES), BF16),
                        pltpu.VMEM((2, tq // 2, tq), F32),
                        pltpu.VMEM((2, tq // 2, tq), F32),
                        pltpu.VMEM((2, 1, tq), F32),
                        pltpu.VMEM((2, 1, tq), F32),
                        pltpu.VMEM((2, V_ROWS, tq), F32),
                        pltpu.VMEM((2, 1, tq), F32)],
        compiler_params=_params("arbitrary", "arbitrary", "arbitrary"),
        name="fox_attention",
    )(qx, kx, vt)


def _outproj_kernel(alpha, yr_ref, ya_ref, za_ref, x_ref, gate_ref, na_ref, w_ref,
                    g_ref, b_ref, o_ref):
    ya = ya_ref[...]
    ms = jnp.mean(ya * ya, axis=-1, keepdims=True)
    za = za_ref[...].astype(F32)
    ya = ya * lax.rsqrt(ms + LN_EPS) * na_ref[...] * (za * jax.nn.sigmoid(za))
    y = (jnp.dot(yr_ref[...], w_ref[0:D_LRU, :], preferred_element_type=F32)
         + jnp.dot(ya.astype(BF16), w_ref[D_LRU:, :], preferred_element_type=F32))
    res = alpha * x_ref[...] + gate_ref[...] * y
    mu = jnp.mean(res, axis=-1, keepdims=True)
    rc = res - mu
    var = jnp.mean(rc * rc, axis=-1, keepdims=True)
    o_ref[...] = rc * lax.rsqrt(var + LN_EPS) * g_ref[...] + b_ref[...]


def _outproj(alpha, yr, ya, za, x, gate, na, w, g, bb):
    b, s, _ = x.shape
    tm = OUT_TILE
    tok = lambda w_: pl.BlockSpec((None, tm, w_), lambda i, t: (i, t, 0))
    full = lambda a: pl.BlockSpec(a.shape, lambda i, t: (0,) * a.ndim)
    vec = pl.BlockSpec((None, 1, D_MODEL), lambda i, t: (i, 0, 0))
    return pl.pallas_call(
        functools.partial(_outproj_kernel, alpha),
        grid=(b, s // tm),
        in_specs=[tok(D_LRU), tok(D_ATTN), tok(D_ATTN), tok(D_MODEL), vec, full(na),
                  full(w), full(g), full(bb)],
        out_specs=tok(D_MODEL),
        out_shape=jax.ShapeDtypeStruct((b, s, D_MODEL), F32),
        compiler_params=_params("arbitrary", "arbitrary"),
        name="outproj_ln",
    )(yr, ya, za, x, gate, na, w, g, bb)


def _bias_placement():
    pq = np.zeros((LANES, D_ATTN), np.float32)
    pk = np.zeros((LANES, D_ATTN), np.float32)
    one = 3 * N_HEADS
    for hd in range(N_HEADS):
        base = (hd // 2) * LANES + (hd % 2) * 6
        for part in range(3):
            src = part * N_HEADS + hd
            pq[src, base + part] = 1.0
            pq[one, base + 3 + part] = 1.0
            pk[one, base + part] = 1.0
            pk[src, base + 3 + part] = -1.0
    return jnp.asarray(pq, BF16), jnp.asarray(pk, BF16)


def _block_diag(w):
    nb, bw, _ = w.shape
    eye = jnp.eye(nb, dtype=w.dtype)
    return (eye[:, None, :, None] * w[:, :, None, :]).reshape(nb * bw, nb * bw)


def kernel(x, c, w_ada, b_ada, w_in, b_fgate, conv_w, conv_b, w_gate_a, b_gate_a,
           w_gate_x, b_gate_x, lru_lambda, norm_lru, norm_attn, w_out, ln_gain, ln_bias):
    depth = w_ada.shape[0]
    batch = x.shape[0]
    alpha = (2.0 * depth) ** 0.25
    n_main = 2 * D_LRU + 4 * D_ATTN

    c8 = jnp.pad(c, ((0, SUBLANES - batch), (0, 0)))
    mod = _adaln(c8, w_ada, b_ada)[:, :batch, :]
    pq, pk = _bias_placement()

    for l in range(depth):
        shift = mod[l, :, 0:D_MODEL].reshape(batch, 1, D_MODEL)
        scale = mod[l, :, D_MODEL:2 * D_MODEL].reshape(batch, 1, D_MODEL)
        gate = mod[l, :, 2 * D_MODEL:].reshape(batch, 1, D_MODEL)

        w_main = w_in[l, :, :n_main].astype(BF16)
        wvt = w_in[l, :, 2 * D_LRU + 2 * D_ATTN:2 * D_LRU + 3 * D_ATTN].T.astype(BF16)
        wf = jnp.zeros((D_MODEL, LANES), F32).at[:, :3 * N_HEADS].set(
            jnp.tile(w_in[l, :, n_main:], (1, 3))).astype(BF16)
        bf = jnp.zeros((1, LANES), F32).at[0, :3 * N_HEADS].set(jnp.tile(b_fgate[l], 3))
        wg = jnp.concatenate([_block_diag(w_gate_a[l]), _block_diag(w_gate_x[l])],
                             axis=1).astype(BF16)
        bg = jnp.concatenate([b_gate_a[l], b_gate_x[l]]).reshape(1, 2 * D_LRU)
        yr, za, qx, kx, vt = _inproj(
            x, scale, shift, w_main, wvt, wf, bf, pq, pk, conv_w[l],
            conv_b[l].reshape(1, D_LRU), wg, bg, lru_lambda[l].reshape(1, D_LRU),
            norm_lru[l].reshape(1, D_LRU))

        ya = _attention(qx, kx, vt)

        x = _outproj(alpha, yr, ya, za, x, gate, norm_attn[l].reshape(1, D_ATTN),
                     w_out[l].astype(BF16), ln_gain[l].reshape(1, D_MODEL),
                     ln_bias[l].reshape(1, D_MODEL))
    return x
```

```python
import functools

import numpy as np
import jax
import jax.numpy as jnp
from jax import lax
from jax.experimental import pallas as pl
from jax.experimental.pallas import tpu as pltpu

F32 = jnp.float32
BF16 = jnp.bfloat16

D_MODEL = 1024
D_LRU = 512
D_ATTN = 512
N_HEADS = 8
HEAD_DIM = 64
N_PAIRS = N_HEADS // 2
LRU_BLOCKS = 8
CONV_W = 4
LRU_C = 8.0
LN_EPS = 1e-5
MASK_VALUE = -1e30
TINY = 1e-37
LOG2E = 1.4426950408889634

SUBLANES = 8
LANES = 128
BF16_ROWS = 16
V_ROWS = HEAD_DIM + BF16_ROWS

TOKEN_TILE = 512
OUT_TILE = 1024
Q_TILE = 1024
ATTN_STRIP = 256
KV_TILE = TOKEN_TILE
VMEM_LIMIT = 56 * 1024 * 1024


def _params(*sem):
    return pltpu.CompilerParams(dimension_semantics=sem, vmem_limit_bytes=VMEM_LIMIT)


def _adaln_kernel(c_ref, w_ref, b_ref, o_ref):
    c = c_ref[...]
    act = c * jax.nn.sigmoid(c)
    o_ref[...] = jnp.dot(act, w_ref[...], precision=lax.Precision.HIGHEST,
                         preferred_element_type=F32) + b_ref[...]


def _adaln(c8, w_ada, b_ada):
    depth, _, n = w_ada.shape
    bn = 1024
    return pl.pallas_call(
        _adaln_kernel,
        grid=(depth, n // bn),
        in_specs=[
            pl.BlockSpec((SUBLANES, D_MODEL), lambda l, j: (0, 0)),
            pl.BlockSpec((None, D_MODEL, bn), lambda l, j: (l, 0, j)),
            pl.BlockSpec((None, 1, bn), lambda l, j: (l, 0, j)),
        ],
        out_specs=pl.BlockSpec((None, SUBLANES, bn), lambda l, j: (l, 0, j)),
        out_shape=jax.ShapeDtypeStruct((depth, SUBLANES, n), F32),
        compiler_params=_params("arbitrary", "arbitrary"),
        name="adaln_mod",
    )(c8, w_ada, b_ada.reshape(depth, 1, n))


def _cumsum_rows(x):
    n = x.shape[0]
    row = lax.broadcasted_iota(jnp.int32, x.shape, 0)
    s = 1
    while s < n:
        x = x + jnp.where(row >= s, pltpu.roll(x, s, 0), 0.0)
        s *= 2
    return x


def _inproj_kernel(x_ref, scale_ref, shift_ref, w_ref, wvt_ref, wf_ref, bf_ref,
                   pq_ref, pk_ref, cw_ref, cb_ref, wg_ref, bg_ref, lam_ref, gn_ref,
                   yr_ref, za_ref, qx_ref, kx_ref, vt_ref,
                   dcarry_ref, xbuf, zbuf, ybuf, h7_s, p7_s, c_s, hcarry):
    tm = x_ref.shape[0]
    n_slab = D_LRU // LANES

    @pl.when(pl.program_id(1) == 0)
    def _():
        dcarry_ref[...] = jnp.zeros_like(dcarry_ref)
        xbuf[:, 0:SUBLANES, :] = jnp.zeros((n_slab, SUBLANES, LANES), F32)
        hcarry[...] = jnp.zeros_like(hcarry)

    x = x_ref[...]
    mu = jnp.mean(x, axis=-1, keepdims=True)
    xc = x - mu
    var = jnp.mean(xc * xc, axis=-1, keepdims=True)
    h = xc * lax.rsqrt(var + LN_EPS) * (1.0 + scale_ref[...]) + shift_ref[...]
    hb = h.astype(BF16)

    half_w = 2 * LANES

    def proj(lo):
        return jnp.dot(hb, w_ref[:, lo:lo + half_w], preferred_element_type=F32)

    def to_slabs(dst, row0, val, half):
        for s2 in range(2):
            dst[2 * half + s2, row0:row0 + tm, :] = val[:, s2 * LANES:(s2 + 1) * LANES]

    for half in range(2):
        to_slabs(xbuf, SUBLANES, proj(half * half_w), half)
    conv = _lru_conv(xbuf, cw_ref, cb_ref, tm)
    for half in range(2):
        to_slabs(zbuf, 0, proj(D_LRU + half * half_w), half)
    g = jnp.dot(conv.astype(BF16), wg_ref[...], preferred_element_type=F32) + bg_ref[...]
    nl = -lam_ref[...]
    sp8 = LRU_C * (jnp.maximum(nl, 0.0) + jnp.log1p(jnp.exp(-jnp.abs(nl))))

    def piece(n):
        kind, half = divmod(n, 2)
        if kind == 0:
            za_ref[:, half * half_w:(half + 1) * half_w] = (
                proj(2 * D_LRU + 3 * D_ATTN + half * half_w).astype(za_ref.dtype))
        elif kind in (1, 2):
            dst, lo, mul = ((qx_ref, 2 * D_LRU, HEAD_DIM ** -0.5 * LOG2E) if kind == 1
                            else (kx_ref, 2 * D_LRU + D_ATTN, None))
            val = proj(lo + half * half_w)
            if mul is not None:
                val = val * mul
            for s2 in range(2):
                p = 2 * half + s2
                dst[:, 2 * p * LANES:(2 * p + 1) * LANES] = (
                    val[:, s2 * LANES:(s2 + 1) * LANES].astype(BF16))
        else:
            vt = lax.dot_general(wvt_ref[half * half_w:(half + 1) * half_w, :], hb,
                                 (((1,), (1,)), ((), ())), preferred_element_type=F32)
            ones = jnp.ones((BF16_ROWS, tm), BF16)
            for h4 in range(N_HEADS // 2):
                hd = half * (N_HEADS // 2) + h4
                vt_ref[hd * V_ROWS:hd * V_ROWS + HEAD_DIM, :] = (
                    vt[h4 * HEAD_DIM:(h4 + 1) * HEAD_DIM, :].astype(BF16))
                vt_ref[hd * V_ROWS + HEAD_DIM:(hd + 1) * V_ROWS, :] = ones

    ng = tm // SUBLANES
    hs, ps = [], []
    for j in range(SUBLANES):
        rows = slice(j * ng, (j + 1) * ng)
        aj, uj = _lru_gates(g[rows, :], conv[rows, :], sp8)
        if j == 0:
            hl, pr = uj, aj
        else:
            hl = aj * hl + uj
            pr = aj * pr
        hs.append(hl)
        ps.append(pr)
        piece(j)

    fl = jnp.dot(hb, wf_ref[...], preferred_element_type=F32) + bf_ref[...]
    logf = jnp.minimum(fl, 0.0) - jnp.log1p(jnp.exp(-jnp.abs(fl)))
    d = _cumsum_rows(logf) + dcarry_ref[0:1, :]
    dcarry_ref[...] = jnp.broadcast_to(d[tm - 1:tm, :], dcarry_ref.shape)
    d2 = d * LOG2E
    hi = d2.astype(BF16).astype(F32)
    r1 = d2 - hi
    mid = r1.astype(BF16).astype(F32)
    lo = (r1 - mid).astype(BF16).astype(F32)
    lane = lax.broadcasted_iota(jnp.int32, d2.shape, 1)
    dsplit = jnp.where(lane < N_HEADS, hi,
             jnp.where(lane < 2 * N_HEADS, mid,
             jnp.where(lane < 3 * N_HEADS, lo,
             jnp.where(lane == 3 * N_HEADS, 1.0, 0.0)))).astype(BF16)
    qbias = jnp.dot(dsplit, pq_ref[...], preferred_element_type=F32)
    kbias = jnp.dot(dsplit, pk_ref[...], preferred_element_type=F32)
    for p in range(N_PAIRS):
        src = slice(p * LANES, (p + 1) * LANES)
        qx_ref[:, (2 * p + 1) * LANES:(2 * p + 2) * LANES] = qbias[:, src].astype(BF16)
        kx_ref[:, (2 * p + 1) * LANES:(2 * p + 2) * LANES] = kbias[:, src].astype(BF16)

    _lru_scan_norm(hs, ps, zbuf, ybuf, gn_ref, h7_s, p7_s, c_s, hcarry, tm)
    for sl in range(n_slab):
        yr_ref[:, sl * LANES:(sl + 1) * LANES] = ybuf[sl].astype(yr_ref.dtype)


def _inproj(x, scale, shift, w_main, wvt, wf, bf, pq, pk, cw, cb, wg, bg, lam, gn):
    b, s, _ = x.shape
    tm = TOKEN_TILE
    nt = s // tm
    ng = tm // SUBLANES
    n_slab = D_LRU // LANES
    tok = lambda w: pl.BlockSpec((None, tm, w), lambda i, t: (i, t, 0))
    full = lambda a: pl.BlockSpec(a.shape, lambda i, t: (0,) * a.ndim)
    vec = pl.BlockSpec((None, 1, D_MODEL), lambda i, t: (i, 0, 0))
    consts = (w_main, wvt, wf, bf, pq, pk, cw, cb, wg, bg, lam, gn)
    return pl.pallas_call(
        _inproj_kernel,
        grid=(b, nt),
        in_specs=[tok(D_MODEL), vec, vec] + [full(a) for a in consts],
        out_specs=[tok(D_LRU), tok(D_ATTN), tok(2 * D_ATTN), tok(2 * D_ATTN),
                   pl.BlockSpec((None, None, N_HEADS * V_ROWS, tm), lambda i, t: (i, t, 0, 0))],
        out_shape=[jax.ShapeDtypeStruct((b, s, D_LRU), BF16),
                   jax.ShapeDtypeStruct((b, s, D_ATTN), BF16),
                   jax.ShapeDtypeStruct((b, s, 2 * D_ATTN), BF16),
                   jax.ShapeDtypeStruct((b, s, 2 * D_ATTN), BF16),
                   jax.ShapeDtypeStruct((b, nt, N_HEADS * V_ROWS, tm), BF16)],
        scratch_shapes=[pltpu.VMEM((SUBLANES, LANES), F32),
                        pltpu.VMEM((n_slab, tm + SUBLANES, LANES), F32),
                        pltpu.VMEM((n_slab, tm, LANES), F32),
                        pltpu.VMEM((n_slab, tm, LANES), F32),
                        pltpu.VMEM((ng, D_LRU), F32),
                        pltpu.VMEM((ng, D_LRU), F32),
                        pltpu.VMEM((ng, D_LRU), F32),
                        pltpu.VMEM((SUBLANES, D_LRU), F32)],
        compiler_params=_params("arbitrary", "arbitrary"),
        name="ln_inproj_rglru",
    )(x, scale, shift, *consts)


def _lru_conv(xbuf, cw_ref, cb_ref, tc):
    ng = tc // SUBLANES
    n_slab = D_LRU // LANES
    lanes = lambda sl: slice(sl * LANES, (sl + 1) * LANES)
    grp = lambda first: pl.ds(first, ng, stride=SUBLANES)
    blocks = []
    for j in range(SUBLANES):
        cols = []
        for sl in range(n_slab):
            acc = cb_ref[:, lanes(sl)]
            for kk in range(CONV_W):
                first = SUBLANES + j - (CONV_W - 1) + kk
                acc = acc + cw_ref[kk:kk + 1, lanes(sl)] * xbuf[sl, grp(first), :]
            cols.append(acc)
        blocks.append(jnp.concatenate(cols, axis=1))
    for sl in range(n_slab):
        xbuf[sl, 0:SUBLANES, :] = xbuf[sl, tc:tc + SUBLANES, :]
    return jnp.concatenate(blocks, axis=0)


def _lru_gates(g, xc, sp8):
    r = jax.nn.sigmoid(g[:, :D_LRU])
    i = jax.nn.sigmoid(g[:, D_LRU:])
    w = r * sp8
    a = jnp.exp(-w)
    m2 = jnp.tanh(w) * (1.0 + a * a)
    mult = m2 * lax.rsqrt(jnp.maximum(m2, TINY))
    return a, mult * (i * xc)


def _lru_scan_norm(hs, ps, zbuf, ybuf, g_ref, h7_s, p7_s, c_s, hcarry, tc):
    ng = tc // SUBLANES
    n_slab = D_LRU // LANES
    lanes = lambda sl: slice(sl * LANES, (sl + 1) * LANES)
    grp = lambda first: pl.ds(first, ng, stride=SUBLANES)
    h7_s[...] = hs[-1]
    p7_s[...] = ps[-1]

    c = hcarry[0:1, :]
    for gi in range(ng):
        c_s[gi:gi + 1, :] = c
        c = p7_s[gi:gi + 1, :] * c + h7_s[gi:gi + 1, :]
    hcarry[0:1, :] = c
    cin = c_s[...]

    for j in range(SUBLANES):
        hj = hs[j] + ps[j] * cin
        ms = jnp.mean(hj * hj, axis=-1, keepdims=True)
        z = jnp.concatenate([zbuf[sl, grp(j), :] for sl in range(n_slab)], axis=1)
        y = hj * lax.rsqrt(ms + LN_EPS) * g_ref[...] * (z * jax.nn.sigmoid(z))
        for sl in range(n_slab):
            ybuf[sl, grp(j), :] = y[:, lanes(sl)]


def _attn_kernel(q_ref, k_ref, v_ref, o_ref, q_s, q_n, s_a, s_b, mx_a, mx_b, acc_ref, m_ref):
    tq = o_ref.shape[0]
    tk = tq // 2
    qi = pl.program_id(2)
    nq = pl.num_programs(2)

    def load_q(t, dst):
        qf = q_ref[pl.ds(pl.multiple_of(t * tq, tq), tq), :].astype(F32)
        lane = lax.broadcasted_iota(jnp.int32, qf.shape, 1)
        nb = 6
        in_a = (lane < HEAD_DIM) | ((lane >= 2 * HEAD_DIM) & (lane < 2 * HEAD_DIM + nb))
        in_b = (((lane >= HEAD_DIM) & (lane < 2 * HEAD_DIM))
                | ((lane >= 2 * HEAD_DIM + nb) & (lane < 2 * HEAD_DIM + 2 * nb)))
        dst[0] = jnp.where(in_a, qf, 0.0).T.astype(BF16)
        dst[1] = jnp.where(in_b, qf, 0.0).T.astype(BF16)

    bufs = ((s_a, mx_a), (s_b, mx_b))

    def skipped(st, half, diag):
        return diag and st * ATTN_STRIP < half * tk

    def score_strip(st, hd, kv, half, diag, q_src=q_s):
        s_ref, mx_ref = bufs[half]
        cols = slice(st * ATTN_STRIP, (st + 1) * ATTN_STRIP)
        start = pl.multiple_of(kv * tq + half * tk, tk)
        k = k_ref[pl.ds(start, tk), :]
        s = jnp.dot(k, q_src[hd, :, cols], preferred_element_type=F32)
        if diag and st * ATTN_STRIP < (half + 1) * tk - 1:
            kpos = half * tk + lax.broadcasted_iota(jnp.int32, s.shape, 0)
            qpos = st * ATTN_STRIP + lax.broadcasted_iota(jnp.int32, s.shape, 1)
            s = jnp.where(kpos <= qpos, s, MASK_VALUE)
        s_ref[hd, :, cols] = s
        mx_ref[hd, :, cols] = jnp.max(s, axis=0, keepdims=True)

    def consume_strip(st, hd, kv, half, diag=False):
        s_ref, mx_ref = bufs[half]
        cols = slice(st * ATTN_STRIP, (st + 1) * ATTN_STRIP)
        m_old = m_ref[hd, :, cols]
        m_new = jnp.maximum(m_old, mx_ref[hd, :, cols])
        alpha = jnp.exp2(m_old - m_new)
        p = jnp.exp2(s_ref[hd, :, cols] - m_new).astype(BF16)
        rows = slice(hd * V_ROWS, (hd + 1) * V_ROWS)
        if v_ref.shape[-1] == tk:
            vt = v_ref[2 * kv + half, rows, :]
        else:
            vt = v_ref[kv, rows, half * tk:(half + 1) * tk]
        acc_ref[hd, :, cols] = (alpha * acc_ref[hd, :, cols]
                                + jnp.dot(vt, p, preferred_element_type=F32))
        m_ref[hd, :, cols] = m_new

    def step(score_args, consume_args):
        for st in range(tq // ATTN_STRIP):
            for hd in range(2):
                if score_args is not None and not skipped(st, *score_args[1:3]):
                    score_strip(st, hd, *score_args)
                if consume_args is not None and not skipped(st, *consume_args[1:3]):
                    consume_strip(st, hd, *consume_args)

    m_ref[...] = jnp.full(m_ref.shape, MASK_VALUE, F32)
    acc_ref[...] = jnp.zeros_like(acc_ref)

    @pl.when(qi == 0)
    def _():
        load_q(0, q_s)
        step((0, 0, True), None)

    @pl.when(qi > 0)
    def _():
        q_s[...] = q_n[...]

    def full_step(kv, carry):
        step((kv, 1, False), (kv, 0, False))
        step((kv + 1, 0, False), (kv, 1, False))
        return carry

    lax.fori_loop(0, qi - 1, full_step, 0)

    @pl.when(qi > 0)
    def _():
        step((qi - 1, 1, False), (qi - 1, 0, False))
        step((qi, 0, True), (qi - 1, 1, False))

    def drain(prefetch_next):
        step((qi, 1, True), (qi, 0, False))
        if prefetch_next:
            load_q(qi + 1, q_n)
            step((0, 0, False, q_n), (qi, 1, True))
        else:
            step(None, (qi, 1, True))

    @pl.when(qi + 1 < nq)
    def _():
        drain(True)

    @pl.when(qi + 1 == nq)
    def _():
        drain(False)

    outs = []
    for hd in range(2):
        acc = acc_ref[hd]
        outs.append(acc[:HEAD_DIM, :] / acc[HEAD_DIM:HEAD_DIM + 1, :])
    o_ref[...] = jnp.concatenate(outs, axis=0).T


def _attention(qx, kx, vt):
    b, s, _ = qx.shape
    nkv, _, tk = vt.shape[1:]
    tq = Q_TILE
    assert tq // 2 in (tk, tk // 2)
    return pl.pallas_call(
        _attn_kernel,
        grid=(b, N_PAIRS, s // tq),
        in_specs=[
            pl.BlockSpec((None, s, 2 * LANES), lambda i, p, t: (i, 0, p)),
            pl.BlockSpec((None, s, 2 * LANES), lambda i, p, t: (i, 0, p)),
            pl.BlockSpec((None, nkv, 2 * V_ROWS, tk), lambda i, p, t: (i, 0, p, 0)),
        ],
        out_specs=pl.BlockSpec((None, tq, LANES), lambda i, p, t: (i, t, p)),
        out_shape=jax.ShapeDtypeStruct((b, s, D_ATTN), F32),
        scratch_shapes=[pltpu.VMEM((2, 2 * LANES, tq), BF16),
                        pltpu.VMEM((2, 2 * LANES, tq), BF16),
                        pltpu.VMEM((2, tq // 2, tq), F32),
                        pltpu.VMEM((2, tq // 2, tq), F32),
                        pltpu.VMEM((2, 1, tq), F32),
                        pltpu.VMEM((2, 1, tq), F32),
                        pltpu.VMEM((2, V_ROWS, tq), F32),
                        pltpu.VMEM((2, 1, tq), F32)],
        compiler_params=_params("arbitrary", "arbitrary", "arbitrary"),
        name="fox_attention",
    )(qx, kx, vt)


def _outproj_kernel(alpha, yr_ref, ya_ref, za_ref, x_ref, gate_ref, na_ref, w_ref,
                    g_ref, b_ref, o_ref):
    ya = ya_ref[...]
    ms = jnp.mean(ya * ya, axis=-1, keepdims=True)
    za = za_ref[...].astype(F32)
    ya = ya * lax.rsqrt(ms + LN_EPS) * na_ref[...] * (za * jax.nn.sigmoid(za))
    y = (jnp.dot(yr_ref[...], w_ref[0:D_LRU, :], preferred_element_type=F32)
         + jnp.dot(ya.astype(BF16), w_ref[D_LRU:, :], preferred_element_type=F32))
    res = alpha * x_ref[...] + gate_ref[...] * y
    mu = jnp.mean(res, axis=-1, keepdims=True)
    rc = res - mu
    var = jnp.mean(rc * rc, axis=-1, keepdims=True)
    o_ref[...] = rc * lax.rsqrt(var + LN_EPS) * g_ref[...] + b_ref[...]


def _outproj(alpha, yr, ya, za, x, gate, na, w, g, bb):
    b, s, _ = x.shape
    tm = OUT_TILE
    tok = lambda w_: pl.BlockSpec((None, tm, w_), lambda i, t: (i, t, 0))
    full = lambda a: pl.BlockSpec(a.shape, lambda i, t: (0,) * a.ndim)
    vec = pl.BlockSpec((None, 1, D_MODEL), lambda i, t: (i, 0, 0))
    return pl.pallas_call(
        functools.partial(_outproj_kernel, alpha),
        grid=(b, s // tm),
        in_specs=[tok(D_LRU), tok(D_ATTN), tok(D_ATTN), tok(D_MODEL), vec, full(na),
                  full(w), full(g), full(bb)],
        out_specs=tok(D_MODEL),
        out_shape=jax.ShapeDtypeStruct((b, s, D_MODEL), F32),
        compiler_params=_params("arbitrary", "arbitrary"),
        name="outproj_ln",
    )(yr, ya, za, x, gate, na, w, g, bb)


def _bias_placement():
    pq = np.zeros((LANES, D_ATTN), np.float32)
    pk = np.zeros((LANES, D_ATTN), np.float32)
    one = 3 * N_HEADS
    for hd in range(N_HEADS):
        base = (hd // 2) * LANES + (hd % 2) * 6
        for part in range(3):
            src = part * N_HEADS + hd
            pq[src, base + part] = 1.0
            pq[one, base + 3 + part] = 1.0
            pk[one, base + part] = 1.0
            pk[src, base + 3 + part] = -1.0
    return jnp.asarray(pq, BF16), jnp.asarray(pk, BF16)


def _block_diag(w):
    nb, bw, _ = w.shape
    eye = jnp.eye(nb, dtype=w.dtype)
    return (eye[:, None, :, None] * w[:, :, None, :]).reshape(nb * bw, nb * bw)


def kernel(x, c, w_ada, b_ada, w_in, b_fgate, conv_w, conv_b, w_gate_a, b_gate_a,
           w_gate_x, b_gate_x, lru_lambda, norm_lru, norm_attn, w_out, ln_gain, ln_bias):
    depth = w_ada.shape[0]
    batch = x.shape[0]
    alpha = (2.0 * depth) ** 0.25
    n_main = 2 * D_LRU + 4 * D_ATTN

    c8 = jnp.pad(c, ((0, SUBLANES - batch), (0, 0)))
    mod = _adaln(c8, w_ada, b_ada)[:, :batch, :]
    pq, pk = _bias_placement()

    for l in range(depth):
        shift = mod[l, :, 0:D_MODEL].reshape(batch, 1, D_MODEL)
        scale = mod[l, :, D_MODEL:2 * D_MODEL].reshape(batch, 1, D_MODEL)
        gate = mod[l, :, 2 * D_MODEL:].reshape(batch, 1, D_MODEL)

        w_main = w_in[l, :, :n_main].astype(BF16)
        wvt = w_in[l, :, 2 * D_LRU + 2 * D_ATTN:2 * D_LRU + 3 * D_ATTN].T.astype(BF16)
        wf = jnp.zeros((D_MODEL, LANES), F32).at[:, :3 * N_HEADS].set(
            jnp.tile(w_in[l, :, n_main:], (1, 3))).astype(BF16)
        bf = jnp.zeros((1, LANES), F32).at[0, :3 * N_HEADS].set(jnp.tile(b_fgate[l], 3))
        wg = jnp.concatenate([_block_diag(w_gate_a[l]), _block_diag(w_gate_x[l])],
                             axis=1).astype(BF16)
        bg = jnp.concatenate([b_gate_a[l], b_gate_x[l]]).reshape(1, 2 * D_LRU)
        yr, za, qx, kx, vt = _inproj(
            x, scale, shift, w_main, wvt, wf, bf, pq, pk, conv_w[l],
            conv_b[l].reshape(1, D_LRU), wg, bg, lru_lambda[l].reshape(1, D_LRU),
            norm_lru[l].reshape(1, D_LRU))

        ya = _attention(qx, kx, vt)

        x = _outproj(alpha, yr, ya, za, x, gate, norm_attn[l].reshape(1, D_ATTN),
                     w_out[l].astype(BF16), ln_gain[l].reshape(1, D_MODEL),
                     ln_bias[l].reshape(1, D_MODEL))
    return x
```

```python
import functools

import numpy as np
import jax
import jax.numpy as jnp
from jax import lax
from jax.experimental import pallas as pl
from jax.experimental.pallas import tpu as pltpu

F32 = jnp.float32
BF16 = jnp.bfloat16

D_MODEL = 1024
D_LRU = 512
D_ATTN = 512
N_HEADS = 8
HEAD_DIM = 64
N_PAIRS = N_HEADS // 2
LRU_BLOCKS = 8
CONV_W = 4
LRU_C = 8.0
LN_EPS = 1e-5
MASK_VALUE = -1e30
TINY = 1e-37
LOG2E = 1.4426950408889634

SUBLANES = 8
LANES = 128
BF16_ROWS = 16
V_ROWS = HEAD_DIM + BF16_ROWS

TOKEN_TILE = 512
OUT_TILE = 1024
Q_TILE = 1024
ATTN_STRIP = 256
KV_TILE = TOKEN_TILE
VMEM_LIMIT = 56 * 1024 * 1024


def _params(*sem):
    return pltpu.CompilerParams(dimension_semantics=sem, vmem_limit_bytes=VMEM_LIMIT)


def _adaln_kernel(c_ref, w_ref, b_ref, o_ref):
    c = c_ref[...]
    act = c * jax.nn.sigmoid(c)
    o_ref[...] = jnp.dot(act, w_ref[...], precision=lax.Precision.HIGHEST,
                         preferred_element_type=F32) + b_ref[...]


def _adaln(c8, w_ada, b_ada):
    depth, _, n = w_ada.shape
    bn = 1024
    return pl.pallas_call(
        _adaln_kernel,
        grid=(depth, n // bn),
        in_specs=[
            pl.BlockSpec((SUBLANES, D_MODEL), lambda l, j: (0, 0)),
            pl.BlockSpec((None, D_MODEL, bn), lambda l, j: (l, 0, j)),
            pl.BlockSpec((None, 1, bn), lambda l, j: (l, 0, j)),
        ],
        out_specs=pl.BlockSpec((None, SUBLANES, bn), lambda l, j: (l, 0, j)),
        out_shape=jax.ShapeDtypeStruct((depth, SUBLANES, n), F32),
        compiler_params=_params("arbitrary", "arbitrary"),
        name="adaln_mod",
    )(c8, w_ada, b_ada.reshape(depth, 1, n))


def _cumsum_rows(x):
    n = x.shape[0]
    row = lax.broadcasted_iota(jnp.int32, x.shape, 0)
    s = 1
    while s < n:
        x = x + jnp.where(row >= s, pltpu.roll(x, s, 0), 0.0)
        s *= 2
    return x


def _inproj_kernel(x_ref, scale_ref, shift_ref, w_ref, wf_ref, bf_ref,
                   pq_ref, pk_ref, cw_ref, cb_ref, wg_ref, bg_ref, lam_ref, gn_ref,
                   yr_ref, za_ref, qx_ref, kx_ref, vt_ref,
                   dcarry_ref, xbuf, zbuf, ybuf, h7_s, p7_s, c_s, hcarry):
    tm = x_ref.shape[0]
    n_slab = D_LRU // LANES

    @pl.when(pl.program_id(1) == 0)
    def _():
        dcarry_ref[...] = jnp.zeros_like(dcarry_ref)
        xbuf[:, 0:SUBLANES, :] = jnp.zeros((n_slab, SUBLANES, LANES), F32)
        hcarry[...] = jnp.zeros_like(hcarry)

    x = x_ref[...]
    mu = jnp.mean(x, axis=-1, keepdims=True)
    xc = x - mu
    var = jnp.mean(xc * xc, axis=-1, keepdims=True)
    h = xc * lax.rsqrt(var + LN_EPS) * (1.0 + scale_ref[...]) + shift_ref[...]
    hb = h.astype(BF16)

    half_w = 2 * LANES

    def proj(lo):
        return jnp.dot(hb, w_ref[:, lo:lo + half_w], preferred_element_type=F32)

    def to_slabs(dst, row0, val, half):
        for s2 in range(2):
            dst[2 * half + s2, row0:row0 + tm, :] = val[:, s2 * LANES:(s2 + 1) * LANES]

    for half in range(2):
        to_slabs(xbuf, SUBLANES, proj(half * half_w), half)
    conv = _lru_conv(xbuf, cw_ref, cb_ref, tm)
    for half in range(2):
        to_slabs(zbuf, 0, proj(D_LRU + half * half_w), half)
    g = jnp.dot(conv.astype(BF16), wg_ref[...], preferred_element_type=F32) + bg_ref[...]
    nl = -lam_ref[...]
    sp8 = LRU_C * (jnp.maximum(nl, 0.0) + jnp.log1p(jnp.exp(-jnp.abs(nl))))

    def piece(n):
        kind, half = divmod(n, 2)
        if kind == 0:
            za_ref[:, half * half_w:(half + 1) * half_w] = (
                proj(2 * D_LRU + 3 * D_ATTN + half * half_w).astype(za_ref.dtype))
        elif kind in (1, 2):
            dst, lo, mul = ((qx_ref, 2 * D_LRU, HEAD_DIM ** -0.5 * LOG2E) if kind == 1
                            else (kx_ref, 2 * D_LRU + D_ATTN, None))
            val = proj(lo + half * half_w)
            if mul is not None:
                val = val * mul
            for s2 in range(2):
                p = 2 * half + s2
                dst[:, 2 * p * LANES:(2 * p + 1) * LANES] = (
                    val[:, s2 * LANES:(s2 + 1) * LANES].astype(BF16))
        else:
            vt = proj(2 * D_LRU + 2 * D_ATTN + half * half_w).T
            ones = jnp.ones((BF16_ROWS, tm), BF16)
            for h4 in range(N_HEADS // 2):
                hd = half * (N_HEADS // 2) + h4
                vt_ref[hd * V_ROWS:hd * V_ROWS + HEAD_DIM, :] = (
                    vt[h4 * HEAD_DIM:(h4 + 1) * HEAD_DIM, :].astype(BF16))
                vt_ref[hd * V_ROWS + HEAD_DIM:(hd + 1) * V_ROWS, :] = ones

    ng = tm // SUBLANES
    hs, ps = [], []
    for j in range(SUBLANES):
        rows = slice(j * ng, (j + 1) * ng)
        aj, uj = _lru_gates(g[rows, :], conv[rows, :], sp8)
        if j == 0:
            hl, pr = uj, aj
        else:
            hl = aj * hl + uj
            pr = aj * pr
        hs.append(hl)
        ps.append(pr)
        piece(j)

    fl = jnp.dot(hb, wf_ref[...], preferred_element_type=F32) + bf_ref[...]
    logf = jnp.minimum(fl, 0.0) - jnp.log1p(jnp.exp(-jnp.abs(fl)))
    d = _cumsum_rows(logf) + dcarry_ref[0:1, :]
    dcarry_ref[...] = jnp.broadcast_to(d[tm - 1:tm, :], dcarry_ref.shape)
    d2 = d * LOG2E
    hi = d2.astype(BF16).astype(F32)
    r1 = d2 - hi
    mid = r1.astype(BF16).astype(F32)
    lo = (r1 - mid).astype(BF16).astype(F32)
    lane = lax.broadcasted_iota(jnp.int32, d2.shape, 1)
    dsplit = jnp.where(lane < N_HEADS, hi,
             jnp.where(lane < 2 * N_HEADS, mid,
             jnp.where(lane < 3 * N_HEADS, lo,
             jnp.where(lane == 3 * N_HEADS, 1.0, 0.0)))).astype(BF16)
    qbias = jnp.dot(dsplit, pq_ref[...], preferred_element_type=F32)
    kbias = jnp.dot(dsplit, pk_ref[...], preferred_element_type=F32)
    for p in range(N_PAIRS):
        src = slice(p * LANES, (p + 1) * LANES)
        qx_ref[:, (2 * p + 1) * LANES:(2 * p + 2) * LANES] = qbias[:, src].astype(BF16)
        kx_ref[:, (2 * p + 1) * LANES:(2 * p + 2) * LANES] = kbias[:, src].astype(BF16)

    _lru_scan_norm(hs, ps, zbuf, ybuf, gn_ref, h7_s, p7_s, c_s, hcarry, tm)
    for sl in range(n_slab):
        yr_ref[:, sl * LANES:(sl + 1) * LANES] = ybuf[sl].astype(yr_ref.dtype)


def _layer_spec(a, l):
    return pl.BlockSpec((None,) + a.shape[1:], lambda i, t: (l,) + (0,) * (a.ndim - 1))


def _mod_spec(l, part):
    return pl.BlockSpec((None, None, 1, D_MODEL), lambda i, t: (l, i, 0, part))


def _inproj(l, x, mod, w_all, wf, bf, pq, pk, cw, cb, wg, bg, lam, gn):
    b, s, _ = x.shape
    tm = TOKEN_TILE
    nt = s // tm
    ng = tm // SUBLANES
    n_slab = D_LRU // LANES
    tok = lambda w: pl.BlockSpec((None, tm, w), lambda i, t: (i, t, 0))
    full = lambda a: pl.BlockSpec(a.shape, lambda i, t: (0,) * a.ndim)
    layered = (w_all, wf, bf)
    return pl.pallas_call(
        _inproj_kernel,
        grid=(b, nt),
        in_specs=([tok(D_MODEL), _mod_spec(l, 1), _mod_spec(l, 0)]
                  + [_layer_spec(a, l) for a in layered] + [full(pq), full(pk)]
                  + [_layer_spec(a, l) for a in (cw, cb, wg, bg, lam, gn)]),
        out_specs=[tok(D_LRU), tok(D_ATTN), tok(2 * D_ATTN), tok(2 * D_ATTN),
                   pl.BlockSpec((None, None, N_HEADS * V_ROWS, tm), lambda i, t: (i, t, 0, 0))],
        out_shape=[jax.ShapeDtypeStruct((b, s, D_LRU), BF16),
                   jax.ShapeDtypeStruct((b, s, D_ATTN), BF16),
                   jax.ShapeDtypeStruct((b, s, 2 * D_ATTN), BF16),
                   jax.ShapeDtypeStruct((b, s, 2 * D_ATTN), BF16),
                   jax.ShapeDtypeStruct((b, nt, N_HEADS * V_ROWS, tm), BF16)],
        scratch_shapes=[pltpu.VMEM((SUBLANES, LANES), F32),
                        pltpu.VMEM((n_slab, tm + SUBLANES, LANES), F32),
                        pltpu.VMEM((n_slab, tm, LANES), F32),
                        pltpu.VMEM((n_slab, tm, LANES), F32),
                        pltpu.VMEM((ng, D_LRU), F32),
                        pltpu.VMEM((ng, D_LRU), F32),
                        pltpu.VMEM((ng, D_LRU), F32),
                        pltpu.VMEM((SUBLANES, D_LRU), F32)],
        compiler_params=_params("arbitrary", "arbitrary"),
        name="ln_inproj_rglru",
    )(x, mod, mod, w_all, wf, bf, pq, pk, cw, cb, wg, bg, lam, gn)


def _lru_conv(xbuf, cw_ref, cb_ref, tc):
    ng = tc // SUBLANES
    n_slab = D_LRU // LANES
    lanes = lambda sl: slice(sl * LANES, (sl + 1) * LANES)
    grp = lambda first: pl.ds(first, ng, stride=SUBLANES)
    blocks = []
    for j in range(SUBLANES):
        cols = []
        for sl in range(n_slab):
            acc = cb_ref[:, lanes(sl)]
            for kk in range(CONV_W):
                first = SUBLANES + j - (CONV_W - 1) + kk
                acc = acc + cw_ref[kk:kk + 1, lanes(sl)] * xbuf[sl, grp(first), :]
            cols.append(acc)
        blocks.append(jnp.concatenate(cols, axis=1))
    for sl in range(n_slab):
        xbuf[sl, 0:SUBLANES, :] = xbuf[sl, tc:tc + SUBLANES, :]
    return jnp.concatenate(blocks, axis=0)


def _lru_gates(g, xc, sp8):
    r = jax.nn.sigmoid(g[:, :D_LRU])
    i = jax.nn.sigmoid(g[:, D_LRU:])
    w = r * sp8
    a = jnp.exp(-w)
    m2 = jnp.tanh(w) * (1.0 + a * a)
    mult = m2 * lax.rsqrt(jnp.maximum(m2, TINY))
    return a, mult * (i * xc)


def _lru_scan_norm(hs, ps, zbuf, ybuf, g_ref, h7_s, p7_s, c_s, hcarry, tc):
    ng = tc // SUBLANES
    n_slab = D_LRU // LANES
    lanes = lambda sl: slice(sl * LANES, (sl + 1) * LANES)
    grp = lambda first: pl.ds(first, ng, stride=SUBLANES)
    h7_s[...] = hs[-1]
    p7_s[...] = ps[-1]

    c = hcarry[0:1, :]
    for gi in range(ng):
        c_s[gi:gi + 1, :] = c
        c = p7_s[gi:gi + 1, :] * c + h7_s[gi:gi + 1, :]
    hcarry[0:1, :] = c
    cin = c_s[...]

    for j in range(SUBLANES):
        hj = hs[j] + ps[j] * cin
        ms = jnp.mean(hj * hj, axis=-1, keepdims=True)
        z = jnp.concatenate([zbuf[sl, grp(j), :] for sl in range(n_slab)], axis=1)
        y = hj * lax.rsqrt(ms + LN_EPS) * g_ref[...] * (z * jax.nn.sigmoid(z))
        for sl in range(n_slab):
            ybuf[sl, grp(j), :] = y[:, lanes(sl)]


def _attn_kernel(q_ref, k_ref, v_ref, o_ref, q_s, q_n, s_a, s_b, mx_a, mx_b, acc_ref, m_ref):
    tq = o_ref.shape[0]
    tk = tq // 2
    qi = pl.program_id(2)
    nq = pl.num_programs(2)

    def load_q(t, dst):
        qf = q_ref[pl.ds(pl.multiple_of(t * tq, tq), tq), :].astype(F32)
        lane = lax.broadcasted_iota(jnp.int32, qf.shape, 1)
        nb = 6
        in_a = (lane < HEAD_DIM) | ((lane >= 2 * HEAD_DIM) & (lane < 2 * HEAD_DIM + nb))
        in_b = (((lane >= HEAD_DIM) & (lane < 2 * HEAD_DIM))
                | ((lane >= 2 * HEAD_DIM + nb) & (lane < 2 * HEAD_DIM + 2 * nb)))
        dst[0] = jnp.where(in_a, qf, 0.0).T.astype(BF16)
        dst[1] = jnp.where(in_b, qf, 0.0).T.astype(BF16)

    bufs = ((s_a, mx_a), (s_b, mx_b))

    def skipped(st, half, diag):
        return diag and st * ATTN_STRIP < half * tk

    def score_strip(st, hd, kv, half, diag, q_src=q_s):
        s_ref, mx_ref = bufs[half]
        cols = slice(st * ATTN_STRIP, (st + 1) * ATTN_STRIP)
        start = pl.multiple_of(kv * tq + half * tk, tk)
        k = k_ref[pl.ds(start, tk), :]
        s = jnp.dot(k, q_src[hd, :, cols], preferred_element_type=F32)
        if diag and st * ATTN_STRIP < (half + 1) * tk - 1:
            kpos = half * tk + lax.broadcasted_iota(jnp.int32, s.shape, 0)
            qpos = st * ATTN_STRIP + lax.broadcasted_iota(jnp.int32, s.shape, 1)
            s = jnp.where(kpos <= qpos, s, MASK_VALUE)
        s_ref[hd, :, cols] = s
        mx_ref[hd, :, cols] = jnp.max(s, axis=0, keepdims=True)

    def consume_strip(st, hd, kv, half, diag=False):
        s_ref, mx_ref = bufs[half]
        cols = slice(st * ATTN_STRIP, (st + 1) * ATTN_STRIP)
        m_old = m_ref[hd, :, cols]
        m_new = jnp.maximum(m_old, mx_ref[hd, :, cols])
        alpha = jnp.exp2(m_old - m_new)
        p = jnp.exp2(s_ref[hd, :, cols] - m_new).astype(BF16)
        rows = slice(hd * V_ROWS, (hd + 1) * V_ROWS)
        if v_ref.shape[-1] == tk:
            vt = v_ref[2 * kv + half, rows, :]
        else:
            vt = v_ref[kv, rows, half * tk:(half + 1) * tk]
        acc_ref[hd, :, cols] = (alpha * acc_ref[hd, :, cols]
                                + jnp.dot(vt, p, preferred_element_type=F32))
        m_ref[hd, :, cols] = m_new

    def step(score_args, consume_args):
        for st in range(tq // ATTN_STRIP):
            for hd in range(2):
                if score_args is not None and not skipped(st, *score_args[1:3]):
                    score_strip(st, hd, *score_args)
                if consume_args is not None and not skipped(st, *consume_args[1:3]):
                    consume_strip(st, hd, *consume_args)

    m_ref[...] = jnp.full(m_ref.shape, MASK_VALUE, F32)
    acc_ref[...] = jnp.zeros_like(acc_ref)

    @pl.when(qi == 0)
    def _():
        load_q(0, q_s)
        step((0, 0, True), None)

    @pl.when(qi > 0)
    def _():
        q_s[...] = q_n[...]

    def full_step(kv, carry):
        step((kv, 1, False), (kv, 0, False))
        step((kv + 1, 0, False), (kv, 1, False))
        return carry

    lax.fori_loop(0, qi - 1, full_step, 0)

    @pl.when(qi > 0)
    def _():
        step((qi - 1, 1, False), (qi - 1, 0, False))
        step((qi, 0, True), (qi - 1, 1, False))

    def drain(prefetch_next):
        step((qi, 1, True), (qi, 0, False))
        if prefetch_next:
            load_q(qi + 1, q_n)
            step((0, 0, False, q_n), (qi, 1, True))
        else:
            step(None, (qi, 1, True))

    @pl.when(qi + 1 < nq)
    def _():
        drain(True)

    @pl.when(qi + 1 == nq)
    def _():
        drain(False)

    outs = []
    for hd in range(2):
        acc = acc_ref[hd]
        outs.append(acc[:HEAD_DIM, :] / acc[HEAD_DIM:HEAD_DIM + 1, :])
    o_ref[...] = jnp.concatenate(outs, axis=0).T.astype(o_ref.dtype)


def _attention(qx, kx, vt):
    b, s, _ = qx.shape
    nkv, _, tk = vt.shape[1:]
    tq = Q_TILE
    assert tq // 2 in (tk, tk // 2)
    return pl.pallas_call(
        _attn_kernel,
        grid=(b, N_PAIRS, s // tq),
        in_specs=[
            pl.BlockSpec((None, s, 2 * LANES), lambda i, p, t: (i, 0, p)),
            pl.BlockSpec((None, s, 2 * LANES), lambda i, p, t: (i, 0, p)),
            pl.BlockSpec((None, nkv, 2 * V_ROWS, tk), lambda i, p, t: (i, 0, p, 0)),
        ],
        out_specs=pl.BlockSpec((None, tq, LANES), lambda i, p, t: (i, t, p)),
        out_shape=jax.ShapeDtypeStruct((b, s, D_ATTN), BF16),
        scratch_shapes=[pltpu.VMEM((2, 2 * LANES, tq), BF16),
                        pltpu.VMEM((2, 2 * LANES, tq), BF16),
                        pltpu.VMEM((2, tq // 2, tq), F32),
                        pltpu.VMEM((2, tq // 2, tq), F32),
                        pltpu.VMEM((2, 1, tq), F32),
                        pltpu.VMEM((2, 1, tq), F32),
                        pltpu.VMEM((2, V_ROWS, tq), F32),
                        pltpu.VMEM((2, 1, tq), F32)],
        compiler_params=_params("arbitrary", "arbitrary", "arbitrary"),
        name="fox_attention",
    )(qx, kx, vt)


def _outproj_kernel(alpha, yr_ref, ya_ref, za_ref, x_ref, gate_ref, na_ref, w_ref,
                    g_ref, b_ref, o_ref):
    ya = ya_ref[...].astype(F32)
    ms = jnp.mean(ya * ya, axis=-1, keepdims=True)
    za = za_ref[...].astype(F32)
    ya = ya * lax.rsqrt(ms + LN_EPS) * na_ref[...] * (za * jax.nn.sigmoid(za))
    y = (jnp.dot(yr_ref[...], w_ref[0:D_LRU, :], preferred_element_type=F32)
         + jnp.dot(ya.astype(BF16), w_ref[D_LRU:, :], preferred_element_type=F32))
    res = alpha * x_ref[...] + gate_ref[...] * y
    mu = jnp.mean(res, axis=-1, keepdims=True)
    rc = res - mu
    var = jnp.mean(rc * rc, axis=-1, keepdims=True)
    o_ref[...] = rc * lax.rsqrt(var + LN_EPS) * g_ref[...] + b_ref[...]


def _outproj(l, alpha, yr, ya, za, x, mod, na, w, g, bb):
    b, s, _ = x.shape
    tm = OUT_TILE
    tok = lambda w_: pl.BlockSpec((None, tm, w_), lambda i, t: (i, t, 0))
    return pl.pallas_call(
        functools.partial(_outproj_kernel, alpha),
        grid=(b, s // tm),
        in_specs=[tok(D_LRU), tok(D_ATTN), tok(D_ATTN), tok(D_MODEL), _mod_spec(l, 2)]
                 + [_layer_spec(a, l) for a in (na, w, g, bb)],
        out_specs=tok(D_MODEL),
        out_shape=jax.ShapeDtypeStruct((b, s, D_MODEL), F32),
        compiler_params=_params("arbitrary", "arbitrary"),
        name="outproj_ln",
    )(yr, ya, za, x, mod, na, w, g, bb)


def _bias_placement():
    pq = np.zeros((LANES, D_ATTN), np.float32)
    pk = np.zeros((LANES, D_ATTN), np.float32)
    one = 3 * N_HEADS
    for hd in range(N_HEADS):
        base = (hd // 2) * LANES + (hd % 2) * 6
        for part in range(3):
            src = part * N_HEADS + hd
            pq[src, base + part] = 1.0
            pq[one, base + 3 + part] = 1.0
            pk[one, base + part] = 1.0
            pk[src, base + 3 + part] = -1.0
    return jnp.asarray(pq, BF16), jnp.asarray(pk, BF16)


def _block_diag(w):
    depth, nb, bw, _ = w.shape
    eye = jnp.eye(nb, dtype=w.dtype)
    return (eye[:, None, :, None] * w[:, :, :, None, :]).reshape(depth, nb * bw, nb * bw)


def kernel(x, c, w_ada, b_ada, w_in, b_fgate, conv_w, conv_b, w_gate_a, b_gate_a,
           w_gate_x, b_gate_x, lru_lambda, norm_lru, norm_attn, w_out, ln_gain, ln_bias):
    depth = w_ada.shape[0]
    batch = x.shape[0]
    alpha = (2.0 * depth) ** 0.25
    n_main = 2 * D_LRU + 4 * D_ATTN
    n_gate = 3 * N_HEADS

    c8 = jnp.pad(c, ((0, SUBLANES - batch), (0, 0)))
    mod = _adaln(c8, w_ada, b_ada).reshape(depth, SUBLANES, 1, 3 * D_MODEL)
    pq, pk = _bias_placement()

    row = lambda a: a.reshape(depth, 1, a.shape[-1])
    w_all = w_in.astype(BF16)
    wf = jnp.zeros((depth, D_MODEL, LANES), F32).at[:, :, :n_gate].set(
        jnp.tile(w_in[:, :, n_main:], (1, 1, 3))).astype(BF16)
    bf = jnp.zeros((depth, 1, LANES), F32).at[:, 0, :n_gate].set(jnp.tile(b_fgate, (1, 3)))
    wg = jnp.concatenate([_block_diag(w_gate_a), _block_diag(w_gate_x)], axis=-1).astype(BF16)
    bg = row(jnp.concatenate([b_gate_a, b_gate_x], axis=-1))
    w_o = w_out.astype(BF16)
    cb, lam, gn, na, lg, lb = (row(a) for a in (conv_b, lru_lambda, norm_lru, norm_attn,
                                                 ln_gain, ln_bias))

    for l in range(depth):
        yr, za, qx, kx, vt = _inproj(l, x, mod, w_all, wf, bf, pq, pk, conv_w, cb, wg, bg,
                                     lam, gn)
        ya = _attention(qx, kx, vt)
        x = _outproj(l, alpha, yr, ya, za, x, mod, na, w_o, lg, lb)
    return x
```

```python
import functools

import numpy as np
import jax
import jax.numpy as jnp
from jax import lax
from jax.experimental import pallas as pl
from jax.experimental.pallas import tpu as pltpu

F32 = jnp.float32
BF16 = jnp.bfloat16

D_MODEL = 1024
D_LRU = 512
D_ATTN = 512
N_HEADS = 8
HEAD_DIM = 64
N_PAIRS = N_HEADS // 2
LRU_BLOCKS = 8
CONV_W = 4
LRU_C = 8.0
LN_EPS = 1e-5
MASK_VALUE = -1e30
TINY = 1e-37
LOG2E = 1.4426950408889634

SUBLANES = 8
LANES = 128
BF16_ROWS = 16
V_ROWS = HEAD_DIM + BF16_ROWS

TOKEN_TILE = 512
OUT_TILE = 1024
Q_TILE = 1024
ATTN_STRIP = 256
KV_TILE = TOKEN_TILE
VMEM_LIMIT = 56 * 1024 * 1024


def _params(*sem):
    return pltpu.CompilerParams(dimension_semantics=sem, vmem_limit_bytes=VMEM_LIMIT)


def _adaln_kernel(c_ref, w_ref, b_ref, o_ref):
    c = c_ref[...]
    act = c * jax.nn.sigmoid(c)
    o_ref[...] = jnp.dot(act, w_ref[...], precision=lax.Precision.HIGHEST,
                         preferred_element_type=F32) + b_ref[...]


def _adaln(c8, w_ada, b_ada):
    depth, _, n = w_ada.shape
    bn = 1024
    return pl.pallas_call(
        _adaln_kernel,
        grid=(depth, n // bn),
        in_specs=[
            pl.BlockSpec((SUBLANES, D_MODEL), lambda l, j: (0, 0)),
            pl.BlockSpec((None, D_MODEL, bn), lambda l, j: (l, 0, j)),
            pl.BlockSpec((None, 1, bn), lambda l, j: (l, 0, j)),
        ],
        out_specs=pl.BlockSpec((None, SUBLANES, bn), lambda l, j: (l, 0, j)),
        out_shape=jax.ShapeDtypeStruct((depth, SUBLANES, n), F32),
        compiler_params=_params("arbitrary", "arbitrary"),
        name="adaln_mod",
    )(c8, w_ada, b_ada.reshape(depth, 1, n))


def _cumsum_rows(x):
    n = x.shape[0]
    row = lax.broadcasted_iota(jnp.int32, x.shape, 0)
    s = 1
    while s < n:
        x = x + jnp.where(row >= s, pltpu.roll(x, s, 0), 0.0)
        s *= 2
    return x


def _inproj_kernel(x_ref, scale_ref, shift_ref, w_ref, wf_ref, bf_ref,
                   pq_ref, pk_ref, cw_ref, cb_ref, wg_ref, bg_ref, lam_ref, gn_ref,
                   yr_ref, za_ref, qx_ref, kx_ref, vt_ref,
                   dcarry_ref, xbuf, zbuf, ybuf, h7_s, p7_s, c_s, hcarry):
    tm = x_ref.shape[0]
    n_slab = D_LRU // LANES

    @pl.when(pl.program_id(1) == 0)
    def _():
        dcarry_ref[...] = jnp.zeros_like(dcarry_ref)
        xbuf[:, 0:SUBLANES, :] = jnp.zeros((n_slab, SUBLANES, LANES), F32)
        hcarry[...] = jnp.zeros_like(hcarry)

    x = x_ref[...]
    mu = jnp.mean(x, axis=-1, keepdims=True)
    xc = x - mu
    var = jnp.mean(xc * xc, axis=-1, keepdims=True)
    h = xc * lax.rsqrt(var + LN_EPS) * (1.0 + scale_ref[...]) + shift_ref[...]
    hb = h.astype(BF16)

    half_w = 2 * LANES

    def proj(lo):
        return jnp.dot(hb, w_ref[:, lo:lo + half_w], preferred_element_type=F32)

    def to_slabs(dst, row0, val, half):
        for s2 in range(2):
            dst[2 * half + s2, row0:row0 + tm, :] = val[:, s2 * LANES:(s2 + 1) * LANES]

    for half in range(2):
        to_slabs(xbuf, SUBLANES, proj(half * half_w), half)
    conv = _lru_conv(xbuf, cw_ref, cb_ref, tm)
    for half in range(2):
        to_slabs(zbuf, 0, proj(D_LRU + half * half_w), half)
    g = jnp.dot(conv.astype(BF16), wg_ref[...], preferred_element_type=F32) + bg_ref[...]
    nl = -lam_ref[...]
    sp8 = LRU_C * (jnp.maximum(nl, 0.0) + jnp.log1p(jnp.exp(-jnp.abs(nl))))

    def piece(n):
        kind, half = divmod(n, 2)
        if kind == 0:
            za_ref[:, half * half_w:(half + 1) * half_w] = (
                proj(2 * D_LRU + 3 * D_ATTN + half * half_w).astype(za_ref.dtype))
        elif kind in (1, 2):
            dst, lo, mul = ((qx_ref, 2 * D_LRU, HEAD_DIM ** -0.5 * LOG2E) if kind == 1
                            else (kx_ref, 2 * D_LRU + D_ATTN, None))
            val = proj(lo + half * half_w)
            if mul is not None:
                val = val * mul
            for s2 in range(2):
                p = 2 * half + s2
                dst[:, 2 * p * LANES:(2 * p + 1) * LANES] = (
                    val[:, s2 * LANES:(s2 + 1) * LANES].astype(BF16))
        else:
            vt = proj(2 * D_LRU + 2 * D_ATTN + half * half_w).T
            ones = jnp.ones((BF16_ROWS, tm), BF16)
            for h4 in range(N_HEADS // 2):
                hd = half * (N_HEADS // 2) + h4
                vt_ref[hd * V_ROWS:hd * V_ROWS + HEAD_DIM, :] = (
                    vt[h4 * HEAD_DIM:(h4 + 1) * HEAD_DIM, :].astype(BF16))
                vt_ref[hd * V_ROWS + HEAD_DIM:(hd + 1) * V_ROWS, :] = ones

    ng = tm // SUBLANES
    hs, ps = [], []
    for j in range(SUBLANES):
        rows = slice(j * ng, (j + 1) * ng)
        aj, uj = _lru_gates(g[rows, :], conv[rows, :], sp8)
        if j == 0:
            hl, pr = uj, aj
        else:
            hl = aj * hl + uj
            pr = aj * pr
        hs.append(hl)
        ps.append(pr)
        piece(j)

    fl = jnp.dot(hb, wf_ref[...], preferred_element_type=F32) + bf_ref[...]
    logf = jnp.minimum(fl, 0.0) - jnp.log1p(jnp.exp(-jnp.abs(fl)))
    d = _cumsum_rows(logf) + dcarry_ref[0:1, :]
    dcarry_ref[...] = jnp.broadcast_to(d[tm - 1:tm, :], dcarry_ref.shape)
    d2 = d * LOG2E
    hi = d2.astype(BF16).astype(F32)
    r1 = d2 - hi
    mid = r1.astype(BF16).astype(F32)
    lo = (r1 - mid).astype(BF16).astype(F32)
    lane = lax.broadcasted_iota(jnp.int32, d2.shape, 1)
    dsplit = jnp.where(lane < N_HEADS, hi,
             jnp.where(lane < 2 * N_HEADS, mid,
             jnp.where(lane < 3 * N_HEADS, lo,
             jnp.where(lane == 3 * N_HEADS, 1.0, 0.0)))).astype(BF16)
    qbias = jnp.dot(dsplit, pq_ref[...], preferred_element_type=F32)
    kbias = jnp.dot(dsplit, pk_ref[...], preferred_element_type=F32)
    for p in range(N_PAIRS):
        src = slice(p * LANES, (p + 1) * LANES)
        qx_ref[:, (2 * p + 1) * LANES:(2 * p + 2) * LANES] = qbias[:, src].astype(BF16)
        kx_ref[:, (2 * p + 1) * LANES:(2 * p + 2) * LANES] = kbias[:, src].astype(BF16)

    _lru_scan_norm(hs, ps, zbuf, ybuf, gn_ref, h7_s, p7_s, c_s, hcarry, tm)
    for sl in range(n_slab):
        yr_ref[:, sl * LANES:(sl + 1) * LANES] = ybuf[sl].astype(yr_ref.dtype)


def _layer_spec(a, l):
    return pl.BlockSpec((None,) + a.shape[1:], lambda i, t: (l,) + (0,) * (a.ndim - 1))


def _mod_spec(l, part):
    return pl.BlockSpec((None, None, 1, D_MODEL), lambda i, t: (l, i, 0, part))


def _inproj(l, x, mod, w_all, wf, bf, pq, pk, cw, cb, wg, bg, lam, gn):
    b, s, _ = x.shape
    tm = TOKEN_TILE
    nt = s // tm
    ng = tm // SUBLANES
    n_slab = D_LRU // LANES
    tok = lambda w: pl.BlockSpec((None, tm, w), lambda i, t: (i, t, 0))
    full = lambda a: pl.BlockSpec(a.shape, lambda i, t: (0,) * a.ndim)
    layered = (w_all, wf, bf)
    return pl.pallas_call(
        _inproj_kernel,
        grid=(b, nt),
        in_specs=([tok(D_MODEL), _mod_spec(l, 1), _mod_spec(l, 0)]
                  + [_layer_spec(a, l) for a in layered] + [full(pq), full(pk)]
                  + [_layer_spec(a, l) for a in (cw, cb, wg, bg, lam, gn)]),
        out_specs=[tok(D_LRU), tok(D_ATTN), tok(2 * D_ATTN), tok(2 * D_ATTN),
                   pl.BlockSpec((None, None, N_HEADS * V_ROWS, tm), lambda i, t: (i, t, 0, 0))],
        out_shape=[jax.ShapeDtypeStruct((b, s, D_LRU), BF16),
                   jax.ShapeDtypeStruct((b, s, D_ATTN), BF16),
                   jax.ShapeDtypeStruct((b, s, 2 * D_ATTN), BF16),
                   jax.ShapeDtypeStruct((b, s, 2 * D_ATTN), BF16),
                   jax.ShapeDtypeStruct((b, nt, N_HEADS * V_ROWS, tm), BF16)],
        scratch_shapes=[pltpu.VMEM((SUBLANES, LANES), F32),
                        pltpu.VMEM((n_slab, tm + SUBLANES, LANES), F32),
                        pltpu.VMEM((n_slab, tm, LANES), F32),
                        pltpu.VMEM((n_slab, tm, LANES), F32),
                        pltpu.VMEM((ng, D_LRU), F32),
                        pltpu.VMEM((ng, D_LRU), F32),
                        pltpu.VMEM((ng, D_LRU), F32),
                        pltpu.VMEM((SUBLANES, D_LRU), F32)],
        compiler_params=_params("arbitrary", "arbitrary"),
        name="ln_inproj_rglru",
    )(x, mod, mod, w_all, wf, bf, pq, pk, cw, cb, wg, bg, lam, gn)


def _lru_conv(xbuf, cw_ref, cb_ref, tc):
    ng = tc // SUBLANES
    n_slab = D_LRU // LANES
    lanes = lambda sl: slice(sl * LANES, (sl + 1) * LANES)
    grp = lambda first: pl.ds(first, ng, stride=SUBLANES)
    blocks = []
    for j in range(SUBLANES):
        cols = []
        for sl in range(n_slab):
            acc = cb_ref[:, lanes(sl)]
            for kk in range(CONV_W):
                first = SUBLANES + j - (CONV_W - 1) + kk
                acc = acc + cw_ref[kk:kk + 1, lanes(sl)] * xbuf[sl, grp(first), :]
            cols.append(acc)
        blocks.append(jnp.concatenate(cols, axis=1))
    for sl in range(n_slab):
        xbuf[sl, 0:SUBLANES, :] = xbuf[sl, tc:tc + SUBLANES, :]
    return jnp.concatenate(blocks, axis=0)


def _lru_gates(g, xc, sp8):
    r = jax.nn.sigmoid(g[:, :D_LRU])
    i = jax.nn.sigmoid(g[:, D_LRU:])
    w = r * sp8
    a = jnp.exp(-w)
    m2 = jnp.tanh(w) * (1.0 + a * a)
    mult = m2 * lax.rsqrt(jnp.maximum(m2, TINY))
    return a, mult * (i * xc)


def _lru_scan_norm(hs, ps, zbuf, ybuf, g_ref, h7_s, p7_s, c_s, hcarry, tc):
    ng = tc // SUBLANES
    n_slab = D_LRU // LANES
    lanes = lambda sl: slice(sl * LANES, (sl + 1) * LANES)
    grp = lambda first: pl.ds(first, ng, stride=SUBLANES)
    h7_s[...] = hs[-1]
    p7_s[...] = ps[-1]

    c = hcarry[0:1, :]
    for gi in range(ng):
        c_s[gi:gi + 1, :] = c
        c = p7_s[gi:gi + 1, :] * c + h7_s[gi:gi + 1, :]
    hcarry[0:1, :] = c
    cin = c_s[...]

    for j in range(SUBLANES):
        hj = hs[j] + ps[j] * cin
        ms = jnp.mean(hj * hj, axis=-1, keepdims=True)
        z = jnp.concatenate([zbuf[sl, grp(j), :] for sl in range(n_slab)], axis=1)
        y = hj * lax.rsqrt(ms + LN_EPS) * g_ref[...] * (z * jax.nn.sigmoid(z))
        for sl in range(n_slab):
            ybuf[sl, grp(j), :] = y[:, lanes(sl)]


def _attn_kernel(q_ref, k_ref, v_ref, o_ref, q_s, q_n, s_a, s_b, mx_a, mx_b, acc_ref, m_ref):
    tq = o_ref.shape[0]
    tk = tq // 2
    qi = pl.program_id(2)
    nq = pl.num_programs(2)

    def load_q(t, dst):
        qf = q_ref[pl.ds(pl.multiple_of(t * tq, tq), tq), :].astype(F32)
        lane = lax.broadcasted_iota(jnp.int32, qf.shape, 1)
        nb = 6
        in_a = (lane < HEAD_DIM) | ((lane >= 2 * HEAD_DIM) & (lane < 2 * HEAD_DIM + nb))
        in_b = (((lane >= HEAD_DIM) & (lane < 2 * HEAD_DIM))
                | ((lane >= 2 * HEAD_DIM + nb) & (lane < 2 * HEAD_DIM + 2 * nb)))
        dst[0] = jnp.where(in_a, qf, 0.0).T.astype(BF16)
        dst[1] = jnp.where(in_b, qf, 0.0).T.astype(BF16)

    bufs = ((s_a, mx_a), (s_b, mx_b))

    def skipped(st, half, diag):
        return diag and st * ATTN_STRIP < half * tk

    def score_strip(st, hd, kv, half, diag, q_src=q_s):
        s_ref, mx_ref = bufs[half]
        cols = slice(st * ATTN_STRIP, (st + 1) * ATTN_STRIP)
        start = pl.multiple_of(kv * tq + half * tk, tk)
        k = k_ref[pl.ds(start, tk), :]
        s = jnp.dot(k, q_src[hd, :, cols], preferred_element_type=F32)
        if diag and st * ATTN_STRIP < (half + 1) * tk - 1:
            kpos = half * tk + lax.broadcasted_iota(jnp.int32, s.shape, 0)
            qpos = st * ATTN_STRIP + lax.broadcasted_iota(jnp.int32, s.shape, 1)
            s = jnp.where(kpos <= qpos, s, MASK_VALUE)
        s_ref[hd, :, cols] = s
        mx_ref[hd, :, cols] = jnp.max(s, axis=0, keepdims=True)

    def consume_strip(st, hd, kv, half, diag=False):
        s_ref, mx_ref = bufs[half]
        cols = slice(st * ATTN_STRIP, (st + 1) * ATTN_STRIP)
        m_old = m_ref[hd, :, cols]
        m_new = jnp.maximum(m_old, mx_ref[hd, :, cols])
        alpha = jnp.exp2(m_old - m_new)
        p = jnp.exp2(s_ref[hd, :, cols] - m_new).astype(BF16)
        rows = slice(hd * V_ROWS, (hd + 1) * V_ROWS)
        if v_ref.shape[-1] == tk:
            vt = v_ref[2 * kv + half, rows, :]
        else:
            vt = v_ref[kv, rows, half * tk:(half + 1) * tk]
        acc_ref[hd, :, cols] = (alpha * acc_ref[hd, :, cols]
                                + jnp.dot(vt, p, preferred_element_type=F32))
        m_ref[hd, :, cols] = m_new

    def step(score_args, consume_args):
        for st in range(tq // ATTN_STRIP):
            for hd in range(2):
                if score_args is not None and not skipped(st, *score_args[1:3]):
                    score_strip(st, hd, *score_args)
                if consume_args is not None and not skipped(st, *consume_args[1:3]):
                    consume_strip(st, hd, *consume_args)

    m_ref[...] = jnp.full(m_ref.shape, MASK_VALUE, F32)
    acc_ref[...] = jnp.zeros_like(acc_ref)

    def full_step(kv):
        step((kv, 1, False), (kv, 0, False))
        step((kv + 1, 0, False), (kv, 1, False))

    def finish(with_prev, prefetch_next):
        if with_prev:
            step((qi - 1, 1, False), (qi - 1, 0, False))
            step((qi, 0, True), (qi - 1, 1, False))
        step((qi, 1, True), (qi, 0, False))
        if prefetch_next:
            load_q(qi + 1, q_n)
            step((0, 0, False, q_n), (qi, 1, True))
        else:
            step(None, (qi, 1, True))

    @pl.when(qi == 0)
    def _():
        load_q(0, q_s)
        step((0, 0, True), None)
        finish(False, True)

    @pl.when(qi > 0)
    def _():
        q_s[...] = q_n[...]

    n_full = jnp.maximum(qi - 1, 0)

    def pair_step(i, carry):
        full_step(2 * i)
        full_step(2 * i + 1)
        return carry

    lax.fori_loop(0, n_full // 2, pair_step, 0)

    @pl.when(n_full % 2 == 1)
    def _():
        full_step(n_full - 1)

    @pl.when((qi > 0) & (qi + 1 < nq))
    def _():
        finish(True, True)

    @pl.when((qi > 0) & (qi + 1 == nq))
    def _():
        finish(True, False)

    outs = []
    for hd in range(2):
        acc = acc_ref[hd]
        outs.append(acc[:HEAD_DIM, :] / acc[HEAD_DIM:HEAD_DIM + 1, :])
    o_ref[...] = jnp.concatenate(outs, axis=0).T.astype(o_ref.dtype)


def _attention(qx, kx, vt):
    b, s, _ = qx.shape
    nkv, _, tk = vt.shape[1:]
    tq = Q_TILE
    assert tq // 2 in (tk, tk // 2)
    assert s // tq >= 2
    return pl.pallas_call(
        _attn_kernel,
        grid=(b, N_PAIRS, s // tq),
        in_specs=[
            pl.BlockSpec((None, s, 2 * LANES), lambda i, p, t: (i, 0, p)),
            pl.BlockSpec((None, s, 2 * LANES), lambda i, p, t: (i, 0, p)),
            pl.BlockSpec((None, nkv, 2 * V_ROWS, tk), lambda i, p, t: (i, 0, p, 0)),
        ],
        out_specs=pl.BlockSpec((None, tq, LANES), lambda i, p, t: (i, t, p)),
        out_shape=jax.ShapeDtypeStruct((b, s, D_ATTN), BF16),
        scratch_shapes=[pltpu.VMEM((2, 2 * LANES, tq), BF16),
                        pltpu.VMEM((2, 2 * LANES, tq), BF16),
                        pltpu.VMEM((2, tq // 2, tq), F32),
                        pltpu.VMEM((2, tq // 2, tq), F32),
                        pltpu.VMEM((2, 1, tq), F32),
                        pltpu.VMEM((2, 1, tq), F32),
                        pltpu.VMEM((2, V_ROWS, tq), F32),
                        pltpu.VMEM((2, 1, tq), F32)],
        compiler_params=_params("arbitrary", "arbitrary", "arbitrary"),
        name="fox_attention",
    )(qx, kx, vt)


def _outproj_kernel(alpha, yr_ref, ya_ref, za_ref, x_ref, gate_ref, na_ref, w_ref,
                    g_ref, b_ref, o_ref):
    ya = ya_ref[...].astype(F32)
    ms = jnp.mean(ya * ya, axis=-1, keepdims=True)
    za = za_ref[...].astype(F32)
    ya = ya * lax.rsqrt(ms + LN_EPS) * na_ref[...] * (za * jax.nn.sigmoid(za))
    y = (jnp.dot(yr_ref[...], w_ref[0:D_LRU, :], preferred_element_type=F32)
         + jnp.dot(ya.astype(BF16), w_ref[D_LRU:, :], preferred_element_type=F32))
    res = alpha * x_ref[...] + gate_ref[...] * y
    mu = jnp.mean(res, axis=-1, keepdims=True)
    rc = res - mu
    var = jnp.mean(rc * rc, axis=-1, keepdims=True)
    o_ref[...] = rc * lax.rsqrt(var + LN_EPS) * g_ref[...] + b_ref[...]


def _outproj(l, alpha, yr, ya, za, x, mod, na, w, g, bb):
    b, s, _ = x.shape
    tm = OUT_TILE
    tok = lambda w_: pl.BlockSpec((None, tm, w_), lambda i, t: (i, t, 0))
    return pl.pallas_call(
        functools.partial(_outproj_kernel, alpha),
        grid=(b, s // tm),
        in_specs=[tok(D_LRU), tok(D_ATTN), tok(D_ATTN), tok(D_MODEL), _mod_spec(l, 2)]
                 + [_layer_spec(a, l) for a in (na, w, g, bb)],
        out_specs=tok(D_MODEL),
        out_shape=jax.ShapeDtypeStruct((b, s, D_MODEL), F32),
        compiler_params=_params("arbitrary", "arbitrary"),
        name="outproj_ln",
    )(yr, ya, za, x, mod, na, w, g, bb)


def _bias_placement():
    pq = np.zeros((LANES, D_ATTN), np.float32)
    pk = np.zeros((LANES, D_ATTN), np.float32)
    one = 3 * N_HEADS
    for hd in range(N_HEADS):
        base = (hd // 2) * LANES + (hd % 2) * 6
        for part in range(3):
            src = part * N_HEADS + hd
            pq[src, base + part] = 1.0
            pq[one, base + 3 + part] = 1.0
            pk[one, base + part] = 1.0
            pk[src, base + 3 + part] = -1.0
    return jnp.asarray(pq, BF16), jnp.asarray(pk, BF16)


def _block_diag(w):
    depth, nb, bw, _ = w.shape
    eye = jnp.eye(nb, dtype=w.dtype)
    return (eye[:, None, :, None] * w[:, :, :, None, :]).reshape(depth, nb * bw, nb * bw)


def kernel(x, c, w_ada, b_ada, w_in, b_fgate, conv_w, conv_b, w_gate_a, b_gate_a,
           w_gate_x, b_gate_x, lru_lambda, norm_lru, norm_attn, w_out, ln_gain, ln_bias):
    depth = w_ada.shape[0]
    batch = x.shape[0]
    alpha = (2.0 * depth) ** 0.25
    n_main = 2 * D_LRU + 4 * D_ATTN
    n_gate = 3 * N_HEADS

    c8 = jnp.pad(c, ((0, SUBLANES - batch), (0, 0)))
    mod = _adaln(c8, w_ada, b_ada).reshape(depth, SUBLANES, 1, 3 * D_MODEL)
    pq, pk = _bias_placement()

    row = lambda a: a.reshape(depth, 1, a.shape[-1])
    w_all = w_in[:, :, :n_main].astype(BF16)
    wf = jnp.zeros((depth, D_MODEL, LANES), F32).at[:, :, :n_gate].set(
        jnp.tile(w_in[:, :, n_main:], (1, 1, 3))).astype(BF16)
    bf = jnp.zeros((depth, 1, LANES), F32).at[:, 0, :n_gate].set(jnp.tile(b_fgate, (1, 3)))
    wg = jnp.concatenate([_block_diag(w_gate_a), _block_diag(w_gate_x)], axis=-1).astype(BF16)
    bg = row(jnp.concatenate([b_gate_a, b_gate_x], axis=-1))
    w_o = w_out.astype(BF16)
    cb, lam, gn, na, lg, lb = (row(a) for a in (conv_b, lru_lambda, norm_lru, norm_attn,
                                                 ln_gain, ln_bias))

    for l in range(depth):
        yr, za, qx, kx, vt = _inproj(l, x, mod, w_all, wf, bf, pq, pk, conv_w, cb, wg, bg,
                                     lam, gn)
        ya = _attention(qx, kx, vt)
        x = _outproj(l, alpha, yr, ya, za, x, mod, na, w_o, lg, lb)
    return x
```

```python
import functools

import numpy as np
import jax
import jax.numpy as jnp
from jax import lax
from jax.experimental import pallas as pl
from jax.experimental.pallas import tpu as pltpu

F32 = jnp.float32
BF16 = jnp.bfloat16

D_MODEL = 1024
D_LRU = 512
D_ATTN = 512
N_HEADS = 8
HEAD_DIM = 64
N_PAIRS = N_HEADS // 2
LRU_BLOCKS = 8
CONV_W = 4
LRU_C = 8.0
LN_EPS = 1e-5
MASK_VALUE = -1e30
TINY = 1e-37
LOG2E = 1.4426950408889634

SUBLANES = 8
LANES = 128
BF16_ROWS = 16
V_ROWS = HEAD_DIM + BF16_ROWS

TOKEN_TILE = 512
OUT_TILE = 1024
Q_TILE = 1024
ATTN_STRIP = 256
KV_TILE = TOKEN_TILE
VMEM_LIMIT = 56 * 1024 * 1024


def _params(*sem):
    return pltpu.CompilerParams(dimension_semantics=sem, vmem_limit_bytes=VMEM_LIMIT)


def _adaln_kernel(c_ref, w_ref, b_ref, o_ref):
    c = c_ref[...]
    act = c * jax.nn.sigmoid(c)
    o_ref[...] = jnp.dot(act, w_ref[...], precision=lax.Precision.HIGHEST,
                         preferred_element_type=F32) + b_ref[...]


def _adaln(c8, w_ada, b_ada):
    depth, _, n = w_ada.shape
    bn = 1024
    return pl.pallas_call(
        _adaln_kernel,
        grid=(depth, n // bn),
        in_specs=[
            pl.BlockSpec((SUBLANES, D_MODEL), lambda l, j: (0, 0)),
            pl.BlockSpec((None, D_MODEL, bn), lambda l, j: (l, 0, j)),
            pl.BlockSpec((None, 1, bn), lambda l, j: (l, 0, j)),
        ],
        out_specs=pl.BlockSpec((None, SUBLANES, bn), lambda l, j: (l, 0, j)),
        out_shape=jax.ShapeDtypeStruct((depth, SUBLANES, n), F32),
        compiler_params=_params("arbitrary", "arbitrary"),
        name="adaln_mod",
    )(c8, w_ada, b_ada.reshape(depth, 1, n))


def _cumsum_rows(x):
    n = x.shape[0]
    row = lax.broadcasted_iota(jnp.int32, x.shape, 0)
    s = 1
    while s < n:
        x = x + jnp.where(row >= s, pltpu.roll(x, s, 0), 0.0)
        s *= 2
    return x


def _inproj_kernel(x_ref, scale_ref, shift_ref, w_ref, wf_ref, bf_ref,
                   pq_ref, pk_ref, cw_ref, cb_ref, wg_ref, bg_ref, lam_ref, gn_ref,
                   yr_ref, za_ref, qx_ref, kx_ref, vt_ref,
                   dcarry_ref, xbuf, zbuf, ybuf, h7_s, p7_s, c_s, hcarry):
    tm = x_ref.shape[0]
    n_slab = D_LRU // LANES

    @pl.when(pl.program_id(1) == 0)
    def _():
        dcarry_ref[...] = jnp.zeros_like(dcarry_ref)
        xbuf[:, 0:SUBLANES, :] = jnp.zeros((n_slab, SUBLANES, LANES), F32)
        hcarry[...] = jnp.zeros_like(hcarry)

    x = x_ref[...]
    mu = jnp.mean(x, axis=-1, keepdims=True)
    xc = x - mu
    var = jnp.mean(xc * xc, axis=-1, keepdims=True)
    h = xc * lax.rsqrt(var + LN_EPS) * (1.0 + scale_ref[...]) + shift_ref[...]
    hb = h.astype(BF16)

    half_w = 2 * LANES

    def proj(lo):
        return jnp.dot(hb, w_ref[:, lo:lo + half_w], preferred_element_type=F32)

    def to_slabs(dst, row0, val, half):
        for s2 in range(2):
            dst[2 * half + s2, row0:row0 + tm, :] = val[:, s2 * LANES:(s2 + 1) * LANES]

    for half in range(2):
        to_slabs(xbuf, SUBLANES, proj(half * half_w), half)
    conv = _lru_conv(xbuf, cw_ref, cb_ref, tm)
    for half in range(2):
        to_slabs(zbuf, 0, proj(D_LRU + half * half_w), half)
    g = jnp.dot(conv.astype(BF16), wg_ref[...], preferred_element_type=F32) + bg_ref[...]
    nl = -lam_ref[...]
    sp8 = LRU_C * (jnp.maximum(nl, 0.0) + jnp.log1p(jnp.exp(-jnp.abs(nl))))

    def piece(n):
        kind, half = divmod(n, 2)
        if kind == 0:
            za_ref[:, half * half_w:(half + 1) * half_w] = (
                proj(2 * D_LRU + 3 * D_ATTN + half * half_w).astype(za_ref.dtype))
        elif kind in (1, 2):
            dst, lo, mul = ((qx_ref, 2 * D_LRU, HEAD_DIM ** -0.5 * LOG2E) if kind == 1
                            else (kx_ref, 2 * D_LRU + D_ATTN, None))
            val = proj(lo + half * half_w)
            if mul is not None:
                val = val * mul
            for s2 in range(2):
                p = 2 * half + s2
                dst[:, 2 * p * LANES:(2 * p + 1) * LANES] = (
                    val[:, s2 * LANES:(s2 + 1) * LANES].astype(BF16))
        else:
            vt = proj(2 * D_LRU + 2 * D_ATTN + half * half_w).T
            ones = jnp.ones((BF16_ROWS, tm), BF16)
            for h4 in range(N_HEADS // 2):
                hd = half * (N_HEADS // 2) + h4
                vt_ref[hd * V_ROWS:hd * V_ROWS + HEAD_DIM, :] = (
                    vt[h4 * HEAD_DIM:(h4 + 1) * HEAD_DIM, :].astype(BF16))
                vt_ref[hd * V_ROWS + HEAD_DIM:(hd + 1) * V_ROWS, :] = ones

    ng = tm // SUBLANES
    hs, ps = [], []
    for j in range(SUBLANES):
        rows = slice(j * ng, (j + 1) * ng)
        aj, uj = _lru_gates(g[rows, :], conv[rows, :], sp8)
        if j == 0:
            hl, pr = uj, aj
        else:
            hl = aj * hl + uj
            pr = aj * pr
        hs.append(hl)
        ps.append(pr)
        piece(j)

    fl = jnp.dot(hb, wf_ref[...], preferred_element_type=F32) + bf_ref[...]
    logf = jnp.minimum(fl, 0.0) - jnp.log1p(jnp.exp(-jnp.abs(fl)))
    d = _cumsum_rows(logf) + dcarry_ref[0:1, :]
    dcarry_ref[...] = jnp.broadcast_to(d[tm - 1:tm, :], dcarry_ref.shape)
    d2 = d * LOG2E
    hi = d2.astype(BF16).astype(F32)
    r1 = d2 - hi
    mid = r1.astype(BF16).astype(F32)
    lo = (r1 - mid).astype(BF16).astype(F32)
    lane = lax.broadcasted_iota(jnp.int32, d2.shape, 1)
    dsplit = jnp.where(lane < N_HEADS, hi,
             jnp.where(lane < 2 * N_HEADS, mid,
             jnp.where(lane < 3 * N_HEADS, lo,
             jnp.where(lane == 3 * N_HEADS, 1.0, 0.0)))).astype(BF16)
    qbias = jnp.dot(dsplit, pq_ref[...], preferred_element_type=F32)
    kbias = jnp.dot(dsplit, pk_ref[...], preferred_element_type=F32)
    for p in range(N_PAIRS):
        src = slice(p * LANES, (p + 1) * LANES)
        qx_ref[:, (2 * p + 1) * LANES:(2 * p + 2) * LANES] = qbias[:, src].astype(BF16)
        kx_ref[:, (2 * p + 1) * LANES:(2 * p + 2) * LANES] = kbias[:, src].astype(BF16)

    _lru_scan_norm(hs, ps, zbuf, ybuf, gn_ref, h7_s, p7_s, c_s, hcarry, tm)
    for sl in range(n_slab):
        yr_ref[:, sl * LANES:(sl + 1) * LANES] = ybuf[sl].astype(yr_ref.dtype)


def _layer_spec(a, l):
    return pl.BlockSpec((None,) + a.shape[1:], lambda i, t: (l,) + (0,) * (a.ndim - 1))


def _mod_spec(l, part):
    return pl.BlockSpec((None, None, 1, D_MODEL), lambda i, t: (l, i, 0, part))


def _inproj(l, x, mod, w_all, wf, bf, pq, pk, cw, cb, wg, bg, lam, gn):
    b, s, _ = x.shape
    tm = TOKEN_TILE
    nt = s // tm
    ng = tm // SUBLANES
    n_slab = D_LRU // LANES
    tok = lambda w: pl.BlockSpec((None, tm, w), lambda i, t: (i, t, 0))
    full = lambda a: pl.BlockSpec(a.shape, lambda i, t: (0,) * a.ndim)
    layered = (w_all, wf, bf)
    return pl.pallas_call(
        _inproj_kernel,
        grid=(b, nt),
        in_specs=([tok(D_MODEL), _mod_spec(l, 1), _mod_spec(l, 0)]
                  + [_layer_spec(a, l) for a in layered] + [full(pq), full(pk)]
                  + [_layer_spec(a, l) for a in (cw, cb, wg, bg, lam, gn)]),
        out_specs=[tok(D_LRU), tok(D_ATTN), tok(2 * D_ATTN), tok(2 * D_ATTN),
                   pl.BlockSpec((None, None, N_HEADS * V_ROWS, tm), lambda i, t: (i, t, 0, 0))],
        out_shape=[jax.ShapeDtypeStruct((b, s, D_LRU), BF16),
                   jax.ShapeDtypeStruct((b, s, D_ATTN), BF16),
                   jax.ShapeDtypeStruct((b, s, 2 * D_ATTN), BF16),
                   jax.ShapeDtypeStruct((b, s, 2 * D_ATTN), BF16),
                   jax.ShapeDtypeStruct((b, nt, N_HEADS * V_ROWS, tm), BF16)],
        scratch_shapes=[pltpu.VMEM((SUBLANES, LANES), F32),
                        pltpu.VMEM((n_slab, tm + SUBLANES, LANES), F32),
                        pltpu.VMEM((n_slab, tm, LANES), F32),
                        pltpu.VMEM((n_slab, tm, LANES), F32),
                        pltpu.VMEM((ng, D_LRU), F32),
                        pltpu.VMEM((ng, D_LRU), F32),
                        pltpu.VMEM((ng, D_LRU), F32),
                        pltpu.VMEM((SUBLANES, D_LRU), F32)],
        compiler_params=_params("arbitrary", "arbitrary"),
        name="ln_inproj_rglru",
    )(x, mod, mod, w_all, wf, bf, pq, pk, cw, cb, wg, bg, lam, gn)


def _lru_conv(xbuf, cw_ref, cb_ref, tc):
    ng = tc // SUBLANES
    n_slab = D_LRU // LANES
    lanes = lambda sl: slice(sl * LANES, (sl + 1) * LANES)
    grp = lambda first: pl.ds(first, ng, stride=SUBLANES)
    blocks = []
    for j in range(SUBLANES):
        cols = []
        for sl in range(n_slab):
            acc = cb_ref[:, lanes(sl)]
            for kk in range(CONV_W):
                first = SUBLANES + j - (CONV_W - 1) + kk
                acc = acc + cw_ref[kk:kk + 1, lanes(sl)] * xbuf[sl, grp(first), :]
            cols.append(acc)
        blocks.append(jnp.concatenate(cols, axis=1))
    for sl in range(n_slab):
        xbuf[sl, 0:SUBLANES, :] = xbuf[sl, tc:tc + SUBLANES, :]
    return jnp.concatenate(blocks, axis=0)


def _lru_gates(g, xc, sp8):
    r = jax.nn.sigmoid(g[:, :D_LRU])
    i = jax.nn.sigmoid(g[:, D_LRU:])
    w = r * sp8
    a = jnp.exp(-w)
    m2 = jnp.tanh(w) * (1.0 + a * a)
    mult = m2 * lax.rsqrt(jnp.maximum(m2, TINY))
    return a, mult * (i * xc)


def _lru_scan_norm(hs, ps, zbuf, ybuf, g_ref, h7_s, p7_s, c_s, hcarry, tc):
    ng = tc // SUBLANES
    n_slab = D_LRU // LANES
    lanes = lambda sl: slice(sl * LANES, (sl + 1) * LANES)
    grp = lambda first: pl.ds(first, ng, stride=SUBLANES)
    h7_s[...] = hs[-1]
    p7_s[...] = ps[-1]

    c = hcarry[0:1, :]
    for gi in range(ng):
        c_s[gi:gi + 1, :] = c
        c = p7_s[gi:gi + 1, :] * c + h7_s[gi:gi + 1, :]
    hcarry[0:1, :] = c
    cin = c_s[...]

    for j in range(SUBLANES):
        hj = hs[j] + ps[j] * cin
        ms = jnp.mean(hj * hj, axis=-1, keepdims=True)
        z = jnp.concatenate([zbuf[sl, grp(j), :] for sl in range(n_slab)], axis=1)
        y = hj * lax.rsqrt(ms + LN_EPS) * g_ref[...] * (z * jax.nn.sigmoid(z))
        for sl in range(n_slab):
            ybuf[sl, grp(j), :] = y[:, lanes(sl)]


def _attn_kernel(q_ref, k_ref, v_ref, o_ref, q_s, q_n, s_a, s_b, mx_a, mx_b, acc_ref, m_ref):
    tq = o_ref.shape[0]
    tk = tq // 2
    qi = pl.program_id(2)
    nq = pl.num_programs(2)

    def load_q(t, dst):
        qf = q_ref[pl.ds(pl.multiple_of(t * tq, tq), tq), :].astype(F32)
        lane = lax.broadcasted_iota(jnp.int32, qf.shape, 1)
        nb = 6
        in_a = (lane < HEAD_DIM) | ((lane >= 2 * HEAD_DIM) & (lane < 2 * HEAD_DIM + nb))
        in_b = (((lane >= HEAD_DIM) & (lane < 2 * HEAD_DIM))
                | ((lane >= 2 * HEAD_DIM + nb) & (lane < 2 * HEAD_DIM + 2 * nb)))
        dst[0] = jnp.where(in_a, qf, 0.0).T.astype(BF16)
        dst[1] = jnp.where(in_b, qf, 0.0).T.astype(BF16)

    bufs = ((s_a, mx_a), (s_b, mx_b))

    def skipped(st, half, diag):
        return diag and st * ATTN_STRIP < half * tk

    def n_keys(st, half, diag):
        return min(tk, (st + 1) * ATTN_STRIP - half * tk) if diag else tk

    def score_strip(st, hd, kv, half, diag, q_src=q_s):
        s_ref, mx_ref = bufs[half]
        cols = slice(st * ATTN_STRIP, (st + 1) * ATTN_STRIP)
        nk = n_keys(st, half, diag)
        start = pl.multiple_of(kv * tq + half * tk, tk)
        k = k_ref[pl.ds(start, nk), :]
        s = jnp.dot(k, q_src[hd, :, cols], preferred_element_type=F32)
        if diag and half * tk + nk - 1 > st * ATTN_STRIP:
            kpos = half * tk + lax.broadcasted_iota(jnp.int32, s.shape, 0)
            qpos = st * ATTN_STRIP + lax.broadcasted_iota(jnp.int32, s.shape, 1)
            s = jnp.where(kpos <= qpos, s, MASK_VALUE)
        s_ref[hd, 0:nk, cols] = s
        mx_ref[hd, :, cols] = jnp.max(s, axis=0, keepdims=True)

    def consume_strip(st, hd, kv, half, diag=False):
        s_ref, mx_ref = bufs[half]
        cols = slice(st * ATTN_STRIP, (st + 1) * ATTN_STRIP)
        nk = n_keys(st, half, diag)
        m_old = m_ref[hd, :, cols]
        m_new = jnp.maximum(m_old, mx_ref[hd, :, cols])
        alpha = jnp.exp2(m_old - m_new)
        p = jnp.exp2(s_ref[hd, 0:nk, cols] - m_new).astype(BF16)
        rows = slice(hd * V_ROWS, (hd + 1) * V_ROWS)
        if v_ref.shape[-1] == tk:
            vt = v_ref[2 * kv + half, rows, 0:nk]
        else:
            vt = v_ref[kv, rows, half * tk:half * tk + nk]
        acc_ref[hd, :, cols] = (alpha * acc_ref[hd, :, cols]
                                + jnp.dot(vt, p, preferred_element_type=F32))
        m_ref[hd, :, cols] = m_new

    def step(score_args, consume_args):
        for st in range(tq // ATTN_STRIP):
            for hd in range(2):
                if score_args is not None and not skipped(st, *score_args[1:3]):
                    score_strip(st, hd, *score_args)
                if consume_args is not None and not skipped(st, *consume_args[1:3]):
                    consume_strip(st, hd, *consume_args)

    m_ref[...] = jnp.full(m_ref.shape, MASK_VALUE, F32)
    acc_ref[...] = jnp.zeros_like(acc_ref)

    def full_step(kv):
        step((kv, 1, False), (kv, 0, False))
        step((kv + 1, 0, False), (kv, 1, False))

    def finish(with_prev, prefetch_next):
        if with_prev:
            step((qi - 1, 1, False), (qi - 1, 0, False))
            step((qi, 0, True), (qi - 1, 1, False))
        step((qi, 1, True), (qi, 0, True))
        if prefetch_next:
            load_q(qi + 1, q_n)
            step((0, 0, False, q_n), (qi, 1, True))
        else:
            step(None, (qi, 1, True))

    @pl.when(qi == 0)
    def _():
        load_q(0, q_s)
        step((0, 0, True), None)
        finish(False, True)

    @pl.when(qi > 0)
    def _():
        q_s[...] = q_n[...]

    n_full = jnp.maximum(qi - 1, 0)

    def pair_step(i, carry):
        full_step(2 * i)
        full_step(2 * i + 1)
        return carry

    lax.fori_loop(0, n_full // 2, pair_step, 0)

    @pl.when(n_full % 2 == 1)
    def _():
        full_step(n_full - 1)

    @pl.when((qi > 0) & (qi + 1 < nq))
    def _():
        finish(True, True)

    @pl.when((qi > 0) & (qi + 1 == nq))
    def _():
        finish(True, False)

    outs = []
    for hd in range(2):
        acc = acc_ref[hd]
        outs.append(acc[:HEAD_DIM, :] / acc[HEAD_DIM:HEAD_DIM + 1, :])
    o_ref[...] = jnp.concatenate(outs, axis=0).T.astype(o_ref.dtype)


def _attention(qx, kx, vt):
    b, s, _ = qx.shape
    nkv, _, tk = vt.shape[1:]
    tq = Q_TILE
    assert tq // 2 in (tk, tk // 2)
    assert s // tq >= 2
    return pl.pallas_call(
        _attn_kernel,
        grid=(b, N_PAIRS, s // tq),
        in_specs=[
            pl.BlockSpec((None, s, 2 * LANES), lambda i, p, t: (i, 0, p)),
            pl.BlockSpec((None, s, 2 * LANES), lambda i, p, t: (i, 0, p)),
            pl.BlockSpec((None, nkv, 2 * V_ROWS, tk), lambda i, p, t: (i, 0, p, 0)),
        ],
        out_specs=pl.BlockSpec((None, tq, LANES), lambda i, p, t: (i, t, p)),
        out_shape=jax.ShapeDtypeStruct((b, s, D_ATTN), BF16),
        scratch_shapes=[pltpu.VMEM((2, 2 * LANES, tq), BF16),
                        pltpu.VMEM((2, 2 * LANES, tq), BF16),
                        pltpu.VMEM((2, tq // 2, tq), F32),
                        pltpu.VMEM((2, tq // 2, tq), F32),
                        pltpu.VMEM((2, 1, tq), F32),
                        pltpu.VMEM((2, 1, tq), F32),
                        pltpu.VMEM((2, V_ROWS, tq), F32),
                        pltpu.VMEM((2, 1, tq), F32)],
        compiler_params=_params("arbitrary", "arbitrary", "arbitrary"),
        name="fox_attention",
    )(qx, kx, vt)


def _outproj_kernel(alpha, yr_ref, ya_ref, za_ref, x_ref, gate_ref, na_ref, w_ref,
                    g_ref, b_ref, o_ref):
    ya = ya_ref[...].astype(F32)
    ms = jnp.mean(ya * ya, axis=-1, keepdims=True)
    za = za_ref[...].astype(F32)
    ya = ya * lax.rsqrt(ms + LN_EPS) * na_ref[...] * (za * jax.nn.sigmoid(za))
    y = (jnp.dot(yr_ref[...], w_ref[0:D_LRU, :], preferred_element_type=F32)
         + jnp.dot(ya.astype(BF16), w_ref[D_LRU:, :], preferred_element_type=F32))
    res = alpha * x_ref[...] + gate_ref[...] * y
    mu = jnp.mean(res, axis=-1, keepdims=True)
    rc = res - mu
    var = jnp.mean(rc * rc, axis=-1, keepdims=True)
    o_ref[...] = rc * lax.rsqrt(var + LN_EPS) * g_ref[...] + b_ref[...]


def _outproj(l, alpha, yr, ya, za, x, mod, na, w, g, bb):
    b, s, _ = x.shape
    tm = OUT_TILE
    tok = lambda w_: pl.BlockSpec((None, tm, w_), lambda i, t: (i, t, 0))
    return pl.pallas_call(
        functools.partial(_outproj_kernel, alpha),
        grid=(b, s // tm),
        in_specs=[tok(D_LRU), tok(D_ATTN), tok(D_ATTN), tok(D_MODEL), _mod_spec(l, 2)]
                 + [_layer_spec(a, l) for a in (na, w, g, bb)],
        out_specs=tok(D_MODEL),
        out_shape=jax.ShapeDtypeStruct((b, s, D_MODEL), F32),
        compiler_params=_params("arbitrary", "arbitrary"),
        name="outproj_ln",
    )(yr, ya, za, x, mod, na, w, g, bb)


def _bias_placement():
    pq = np.zeros((LANES, D_ATTN), np.float32)
    pk = np.zeros((LANES, D_ATTN), np.float32)
    one = 3 * N_HEADS
    for hd in range(N_HEADS):
        base = (hd // 2) * LANES + (hd % 2) * 6
        for part in range(3):
            src = part * N_HEADS + hd
            pq[src, base + part] = 1.0
            pq[one, base + 3 + part] = 1.0
            pk[one, base + part] = 1.0
            pk[src, base + 3 + part] = -1.0
    return jnp.asarray(pq, BF16), jnp.asarray(pk, BF16)


def _block_diag(w):
    depth, nb, bw, _ = w.shape
    eye = jnp.eye(nb, dtype=w.dtype)
    return (eye[:, None, :, None] * w[:, :, :, None, :]).reshape(depth, nb * bw, nb * bw)


def kernel(x, c, w_ada, b_ada, w_in, b_fgate, conv_w, conv_b, w_gate_a, b_gate_a,
           w_gate_x, b_gate_x, lru_lambda, norm_lru, norm_attn, w_out, ln_gain, ln_bias):
    depth = w_ada.shape[0]
    batch = x.shape[0]
    alpha = (2.0 * depth) ** 0.25
    n_main = 2 * D_LRU + 4 * D_ATTN
    n_gate = 3 * N_HEADS

    c8 = jnp.pad(c, ((0, SUBLANES - batch), (0, 0)))
    mod = _adaln(c8, w_ada, b_ada).reshape(depth, SUBLANES, 1, 3 * D_MODEL)
    pq, pk = _bias_placement()

    row = lambda a: a.reshape(depth, 1, a.shape[-1])
    w_all = w_in.astype(BF16)[:, :, :n_main]
    wf = jnp.zeros((depth, D_MODEL, LANES), F32).at[:, :, :n_gate].set(
        jnp.tile(w_in[:, :, n_main:], (1, 1, 3))).astype(BF16)
    bf = jnp.zeros((depth, 1, LANES), F32).at[:, 0, :n_gate].set(jnp.tile(b_fgate, (1, 3)))
    wg = jnp.concatenate([_block_diag(w_gate_a), _block_diag(w_gate_x)], axis=-1).astype(BF16)
    bg = row(jnp.concatenate([b_gate_a, b_gate_x], axis=-1))
    w_o = w_out.astype(BF16)
    cb, lam, gn, na, lg, lb = (row(a) for a in (conv_b, lru_lambda, norm_lru, norm_attn,
                                                 ln_gain, ln_bias))

    for l in range(depth):
        yr, za, qx, kx, vt = _inproj(l, x, mod, w_all, wf, bf, pq, pk, conv_w, cb, wg, bg,
                                     lam, gn)
        ya = _attention(qx, kx, vt)
        x = _outproj(l, alpha, yr, ya, za, x, mod, na, w_o, lg, lb)
    return x
```

```python
import functools

import numpy as np
import jax
import jax.numpy as jnp
from jax import lax
from jax.experimental import pallas as pl
from jax.experimental.pallas import tpu as pltpu

F32 = jnp.float32
BF16 = jnp.bfloat16

D_MODEL = 1024
D_LRU = 512
D_ATTN = 512
N_HEADS = 8
HEAD_DIM = 64
N_PAIRS = N_HEADS // 2
LRU_BLOCKS = 8
CONV_W = 4
LRU_C = 8.0
LN_EPS = 1e-5
MASK_VALUE = -1e30
TINY = 1e-37
LOG2E = 1.4426950408889634

SUBLANES = 8
LANES = 128
BF16_ROWS = 16
V_ROWS = HEAD_DIM + BF16_ROWS

TOKEN_TILE = 512
OUT_TILE = 1024
OUT_CHUNK = 256
Q_TILE = 1024
ATTN_STRIP = 256
KV_TILE = TOKEN_TILE
VMEM_LIMIT = 56 * 1024 * 1024


def _params(*sem):
    return pltpu.CompilerParams(dimension_semantics=sem, vmem_limit_bytes=VMEM_LIMIT)


def _adaln_kernel(c_ref, w_ref, b_ref, o_ref):
    c = c_ref[...]
    act = c * jax.nn.sigmoid(c)
    o_ref[...] = jnp.dot(act, w_ref[...], precision=lax.Precision.HIGHEST,
                         preferred_element_type=F32) + b_ref[...]


def _adaln(c8, w_ada, b_ada):
    depth, _, n = w_ada.shape
    bn = 1024
    return pl.pallas_call(
        _adaln_kernel,
        grid=(depth, n // bn),
        in_specs=[
            pl.BlockSpec((SUBLANES, D_MODEL), lambda l, j: (0, 0)),
            pl.BlockSpec((None, D_MODEL, bn), lambda l, j: (l, 0, j)),
            pl.BlockSpec((None, 1, bn), lambda l, j: (l, 0, j)),
        ],
        out_specs=pl.BlockSpec((None, SUBLANES, bn), lambda l, j: (l, 0, j)),
        out_shape=jax.ShapeDtypeStruct((depth, SUBLANES, n), F32),
        compiler_params=_params("arbitrary", "arbitrary"),
        name="adaln_mod",
    )(c8, w_ada, b_ada.reshape(depth, 1, n))


def _cumsum_rows(x):
    n = x.shape[0]
    row = lax.broadcasted_iota(jnp.int32, x.shape, 0)
    s = 1
    while s < n:
        x = x + jnp.where(row >= s, pltpu.roll(x, s, 0), 0.0)
        s *= 2
    return x


def _inproj_kernel(x_ref, scale_ref, shift_ref, w_ref, wf_ref, bf_ref,
                   pq_ref, pk_ref, cw_ref, cb_ref, wg_ref, bg_ref, lam_ref, gn_ref,
                   yr_ref, za_ref, qx_ref, kx_ref, vt_ref,
                   dcarry_ref, xbuf, zbuf, ybuf, h7_s, p7_s, c_s, hcarry):
    tm = x_ref.shape[0]
    n_slab = D_LRU // LANES

    @pl.when(pl.program_id(1) == 0)
    def _():
        dcarry_ref[...] = jnp.zeros_like(dcarry_ref)
        xbuf[:, 0:SUBLANES, :] = jnp.zeros((n_slab, SUBLANES, LANES), F32)
        hcarry[...] = jnp.zeros_like(hcarry)

    half_w = 2 * LANES

    def layer_norm(rows):
        x = x_ref[rows, :]
        mu = jnp.mean(x, axis=-1, keepdims=True)
        xc = x - mu
        var = jnp.mean(xc * xc, axis=-1, keepdims=True)
        h = xc * lax.rsqrt(var + LN_EPS) * (1.0 + scale_ref[...]) + shift_ref[...]
        return h.astype(BF16)

    def to_slabs(dst, row0, val, half):
        for s2 in range(2):
            dst[2 * half + s2, row0:row0 + val.shape[0], :] = val[:, s2 * LANES:(s2 + 1) * LANES]

    hm = tm // 2
    hb_parts = []
    for r in range(2):
        hb_r = layer_norm(slice(r * hm, (r + 1) * hm))
        hb_parts.append(hb_r)
        for half in range(2):
            xr = jnp.dot(hb_r, w_ref[:, half * half_w:(half + 1) * half_w],
                         preferred_element_type=F32)
            to_slabs(xbuf, SUBLANES + r * hm, xr, half)
    hb = jnp.concatenate(hb_parts, axis=0)

    def proj(lo):
        return jnp.dot(hb, w_ref[:, lo:lo + half_w], preferred_element_type=F32)

    conv = _lru_conv(xbuf, cw_ref, cb_ref, tm)
    for half in range(2):
        to_slabs(zbuf, 0, proj(D_LRU + half * half_w), half)
    g = jnp.dot(conv.astype(BF16), wg_ref[...], preferred_element_type=F32) + bg_ref[...]
    nl = -lam_ref[...]
    sp8 = LRU_C * (jnp.maximum(nl, 0.0) + jnp.log1p(jnp.exp(-jnp.abs(nl))))

    def piece(n):
        kind, half = divmod(n, 2)
        if kind == 0:
            za_ref[:, half * half_w:(half + 1) * half_w] = (
                proj(2 * D_LRU + 3 * D_ATTN + half * half_w).astype(za_ref.dtype))
        elif kind in (1, 2):
            dst, lo, mul = ((qx_ref, 2 * D_LRU, HEAD_DIM ** -0.5 * LOG2E) if kind == 1
                            else (kx_ref, 2 * D_LRU + D_ATTN, None))
            val = proj(lo + half * half_w)
            if mul is not None:
                val = val * mul
            for s2 in range(2):
                p = 2 * half + s2
                dst[:, 2 * p * LANES:(2 * p + 1) * LANES] = (
                    val[:, s2 * LANES:(s2 + 1) * LANES].astype(BF16))
        else:
            vt = proj(2 * D_LRU + 2 * D_ATTN + half * half_w).T
            ones = jnp.ones((BF16_ROWS, tm), BF16)
            for h4 in range(N_HEADS // 2):
                hd = half * (N_HEADS // 2) + h4
                vt_ref[hd * V_ROWS:hd * V_ROWS + HEAD_DIM, :] = (
                    vt[h4 * HEAD_DIM:(h4 + 1) * HEAD_DIM, :].astype(BF16))
                vt_ref[hd * V_ROWS + HEAD_DIM:(hd + 1) * V_ROWS, :] = ones

    ng = tm // SUBLANES
    hs, ps = [], []
    for j in range(SUBLANES):
        rows = slice(j * ng, (j + 1) * ng)
        aj, uj = _lru_gates(g[rows, :], conv[rows, :], sp8)
        if j == 0:
            hl, pr = uj, aj
        else:
            hl = aj * hl + uj
            pr = aj * pr
        hs.append(hl)
        ps.append(pr)
        piece(j)

    fl = jnp.dot(hb, wf_ref[...], preferred_element_type=F32) + bf_ref[...]
    logf = jnp.minimum(fl, 0.0) - jnp.log1p(jnp.exp(-jnp.abs(fl)))
    d = _cumsum_rows(logf) + dcarry_ref[0:1, :]
    dcarry_ref[...] = jnp.broadcast_to(d[tm - 1:tm, :], dcarry_ref.shape)
    d2 = d * LOG2E
    hi = d2.astype(BF16).astype(F32)
    r1 = d2 - hi
    mid = r1.astype(BF16).astype(F32)
    lo = (r1 - mid).astype(BF16).astype(F32)
    lane = lax.broadcasted_iota(jnp.int32, d2.shape, 1)
    dsplit = jnp.where(lane < N_HEADS, hi,
             jnp.where(lane < 2 * N_HEADS, mid,
             jnp.where(lane < 3 * N_HEADS, lo,
             jnp.where(lane == 3 * N_HEADS, 1.0, 0.0)))).astype(BF16)
    qbias = jnp.dot(dsplit, pq_ref[...], preferred_element_type=F32)
    kbias = jnp.dot(dsplit, pk_ref[...], preferred_element_type=F32)
    for p in range(N_PAIRS):
        src = slice(p * LANES, (p + 1) * LANES)
        qx_ref[:, (2 * p + 1) * LANES:(2 * p + 2) * LANES] = qbias[:, src].astype(BF16)
        kx_ref[:, (2 * p + 1) * LANES:(2 * p + 2) * LANES] = kbias[:, src].astype(BF16)

    _lru_scan_norm(hs, ps, zbuf, ybuf, gn_ref, h7_s, p7_s, c_s, hcarry, tm)
    for sl in range(n_slab):
        yr_ref[:, sl * LANES:(sl + 1) * LANES] = ybuf[sl].astype(yr_ref.dtype)


def _layer_spec(a, l):
    return pl.BlockSpec((None,) + a.shape[1:], lambda i, t: (l,) + (0,) * (a.ndim - 1))


def _mod_spec(l, part):
    return pl.BlockSpec((None, None, 1, D_MODEL), lambda i, t: (l, i, 0, part))


def _inproj(l, x, mod, w_all, wf, bf, pq, pk, cw, cb, wg, bg, lam, gn):
    b, s, _ = x.shape
    tm = TOKEN_TILE
    nt = s // tm
    ng = tm // SUBLANES
    n_slab = D_LRU // LANES
    tok = lambda w: pl.BlockSpec((None, tm, w), lambda i, t: (i, t, 0))
    full = lambda a: pl.BlockSpec(a.shape, lambda i, t: (0,) * a.ndim)
    layered = (w_all, wf, bf)
    return pl.pallas_call(
        _inproj_kernel,
        grid=(b, nt),
        in_specs=([tok(D_MODEL), _mod_spec(l, 1), _mod_spec(l, 0)]
                  + [_layer_spec(a, l) for a in layered] + [full(pq), full(pk)]
                  + [_layer_spec(a, l) for a in (cw, cb, wg, bg, lam, gn)]),
        out_specs=[tok(D_LRU), tok(D_ATTN), tok(2 * D_ATTN), tok(2 * D_ATTN),
                   pl.BlockSpec((None, None, N_HEADS * V_ROWS, tm), lambda i, t: (i, t, 0, 0))],
        out_shape=[jax.ShapeDtypeStruct((b, s, D_LRU), BF16),
                   jax.ShapeDtypeStruct((b, s, D_ATTN), BF16),
                   jax.ShapeDtypeStruct((b, s, 2 * D_ATTN), BF16),
                   jax.ShapeDtypeStruct((b, s, 2 * D_ATTN), BF16),
                   jax.ShapeDtypeStruct((b, nt, N_HEADS * V_ROWS, tm), BF16)],
        scratch_shapes=[pltpu.VMEM((SUBLANES, LANES), F32),
                        pltpu.VMEM((n_slab, tm + SUBLANES, LANES), F32),
                        pltpu.VMEM((n_slab, tm, LANES), F32),
                        pltpu.VMEM((n_slab, tm, LANES), F32),
                        pltpu.VMEM((ng, D_LRU), F32),
                        pltpu.VMEM((ng, D_LRU), F32),
                        pltpu.VMEM((ng, D_LRU), F32),
                        pltpu.VMEM((SUBLANES, D_LRU), F32)],
        compiler_params=_params("arbitrary", "arbitrary"),
        name="ln_inproj_rglru",
    )(x, mod, mod, w_all, wf, bf, pq, pk, cw, cb, wg, bg, lam, gn)


def _lru_conv(xbuf, cw_ref, cb_ref, tc):
    ng = tc // SUBLANES
    n_slab = D_LRU // LANES
    lanes = lambda sl: slice(sl * LANES, (sl + 1) * LANES)
    grp = lambda first: pl.ds(first, ng, stride=SUBLANES)
    blocks = []
    for j in range(SUBLANES):
        cols = []
        for sl in range(n_slab):
            acc = cb_ref[:, lanes(sl)]
            for kk in range(CONV_W):
                first = SUBLANES + j - (CONV_W - 1) + kk
                acc = acc + cw_ref[kk:kk + 1, lanes(sl)] * xbuf[sl, grp(first), :]
            cols.append(acc)
        blocks.append(jnp.concatenate(cols, axis=1))
    for sl in range(n_slab):
        xbuf[sl, 0:SUBLANES, :] = xbuf[sl, tc:tc + SUBLANES, :]
    return jnp.concatenate(blocks, axis=0)


def _lru_gates(g, xc, sp8):
    r = jax.nn.sigmoid(g[:, :D_LRU])
    i = jax.nn.sigmoid(g[:, D_LRU:])
    w = r * sp8
    a = jnp.exp(-w)
    m2 = jnp.tanh(w) * (1.0 + a * a)
    mult = m2 * lax.rsqrt(jnp.maximum(m2, TINY))
    return a, mult * (i * xc)


def _lru_scan_norm(hs, ps, zbuf, ybuf, g_ref, h7_s, p7_s, c_s, hcarry, tc):
    ng = tc // SUBLANES
    n_slab = D_LRU // LANES
    lanes = lambda sl: slice(sl * LANES, (sl + 1) * LANES)
    grp = lambda first: pl.ds(first, ng, stride=SUBLANES)
    h7_s[...] = hs[-1]
    p7_s[...] = ps[-1]

    c = hcarry[0:1, :]
    for gi in range(ng):
        c_s[gi:gi + 1, :] = c
        c = p7_s[gi:gi + 1, :] * c + h7_s[gi:gi + 1, :]
    hcarry[0:1, :] = c
    cin = c_s[...]

    for j in range(SUBLANES):
        hj = hs[j] + ps[j] * cin
        ms = jnp.mean(hj * hj, axis=-1, keepdims=True)
        z = jnp.concatenate([zbuf[sl, grp(j), :] for sl in range(n_slab)], axis=1)
        y = hj * lax.rsqrt(ms + LN_EPS) * g_ref[...] * (z * jax.nn.sigmoid(z))
        for sl in range(n_slab):
            ybuf[sl, grp(j), :] = y[:, lanes(sl)]


def _attn_kernel(q_ref, k_ref, v_ref, o_ref, q_s, q_n, s_a, s_b, mx_a, mx_b, acc_ref, m_ref):
    tq = o_ref.shape[0]
    tk = tq // 2
    qi = pl.program_id(2)
    nq = pl.num_programs(2)

    def load_q(t, dst):
        qf = q_ref[pl.ds(pl.multiple_of(t * tq, tq), tq), :].astype(F32)
        lane = lax.broadcasted_iota(jnp.int32, qf.shape, 1)
        nb = 6
        in_a = (lane < HEAD_DIM) | ((lane >= 2 * HEAD_DIM) & (lane < 2 * HEAD_DIM + nb))
        in_b = (((lane >= HEAD_DIM) & (lane < 2 * HEAD_DIM))
                | ((lane >= 2 * HEAD_DIM + nb) & (lane < 2 * HEAD_DIM + 2 * nb)))
        dst[0] = jnp.where(in_a, qf, 0.0).T.astype(BF16)
        dst[1] = jnp.where(in_b, qf, 0.0).T.astype(BF16)

    bufs = ((s_a, mx_a), (s_b, mx_b))

    def skipped(st, half, diag):
        return diag and st * ATTN_STRIP < half * tk

    def n_keys(st, half, diag):
        return min(tk, (st + 1) * ATTN_STRIP - half * tk) if diag else tk

    def score_strip(st, hd, kv, half, diag, q_src=q_s):
        s_ref, mx_ref = bufs[half]
        cols = slice(st * ATTN_STRIP, (st + 1) * ATTN_STRIP)
        nk = n_keys(st, half, diag)
        start = pl.multiple_of(kv * tq + half * tk, tk)
        k = k_ref[pl.ds(start, nk), :]
        s = jnp.dot(k, q_src[hd, :, cols], preferred_element_type=F32)
        if diag and half * tk + nk - 1 > st * ATTN_STRIP:
            kpos = half * tk + lax.broadcasted_iota(jnp.int32, s.shape, 0)
            qpos = st * ATTN_STRIP + lax.broadcasted_iota(jnp.int32, s.shape, 1)
            s = jnp.where(kpos <= qpos, s, MASK_VALUE)
        s_ref[hd, 0:nk, cols] = s
        mx_ref[hd, :, cols] = jnp.max(s, axis=0, keepdims=True)

    def consume_strip(st, hd, kv, half, diag=False):
        s_ref, mx_ref = bufs[half]
        cols = slice(st * ATTN_STRIP, (st + 1) * ATTN_STRIP)
        nk = n_keys(st, half, diag)
        m_old = m_ref[hd, :, cols]
        m_new = jnp.maximum(m_old, mx_ref[hd, :, cols])
        alpha = jnp.exp2(m_old - m_new)
        p = jnp.exp2(s_ref[hd, 0:nk, cols] - m_new).astype(BF16)
        rows = slice(hd * V_ROWS, (hd + 1) * V_ROWS)
        if v_ref.shape[-1] == tk:
            vt = v_ref[2 * kv + half, rows, 0:nk]
        else:
            vt = v_ref[kv, rows, half * tk:half * tk + nk]
        acc_ref[hd, :, cols] = (alpha * acc_ref[hd, :, cols]
                                + jnp.dot(vt, p, preferred_element_type=F32))
        m_ref[hd, :, cols] = m_new

    def step(score_args, consume_args):
        for st in range(tq // ATTN_STRIP):
            for hd in range(2):
                if score_args is not None and not skipped(st, *score_args[1:3]):
                    score_strip(st, hd, *score_args)
                if consume_args is not None and not skipped(st, *consume_args[1:3]):
                    consume_strip(st, hd, *consume_args)

    m_ref[...] = jnp.full(m_ref.shape, MASK_VALUE, F32)
    acc_ref[...] = jnp.zeros_like(acc_ref)

    def full_step(kv):
        step((kv, 1, False), (kv, 0, False))
        step((kv + 1, 0, False), (kv, 1, False))

    def finish(with_prev, prefetch_next):
        if with_prev:
            step((qi - 1, 1, False), (qi - 1, 0, False))
            step((qi, 0, True), (qi - 1, 1, False))
        step((qi, 1, True), (qi, 0, True))
        if prefetch_next:
            load_q(qi + 1, q_n)
            step((0, 0, False, q_n), (qi, 1, True))
        else:
            step(None, (qi, 1, True))

    @pl.when(qi == 0)
    def _():
        load_q(0, q_s)
        step((0, 0, True), None)
        finish(False, True)

    @pl.when(qi > 0)
    def _():
        q_s[...] = q_n[...]

    n_full = jnp.maximum(qi - 1, 0)

    def pair_step(i, carry):
        full_step(2 * i)
        full_step(2 * i + 1)
        return carry

    lax.fori_loop(0, n_full // 2, pair_step, 0)

    @pl.when(n_full % 2 == 1)
    def _():
        full_step(n_full - 1)

    @pl.when((qi > 0) & (qi + 1 < nq))
    def _():
        finish(True, True)

    @pl.when((qi > 0) & (qi + 1 == nq))
    def _():
        finish(True, False)

    outs = []
    for hd in range(2):
        acc = acc_ref[hd]
        outs.append(acc[:HEAD_DIM, :] / acc[HEAD_DIM:HEAD_DIM + 1, :])
    o_ref[...] = jnp.concatenate(outs, axis=0).T.astype(o_ref.dtype)


def _attention(qx, kx, vt):
    b, s, _ = qx.shape
    nkv, _, tk = vt.shape[1:]
    tq = Q_TILE
    assert tq // 2 in (tk, tk // 2)
    assert s // tq >= 2
    return pl.pallas_call(
        _attn_kernel,
        grid=(b, N_PAIRS, s // tq),
        in_specs=[
            pl.BlockSpec((None, s, 2 * LANES), lambda i, p, t: (i, 0, p)),
            pl.BlockSpec((None, s, 2 * LANES), lambda i, p, t: (i, 0, p)),
            pl.BlockSpec((None, nkv, 2 * V_ROWS, tk), lambda i, p, t: (i, 0, p, 0)),
        ],
        out_specs=pl.BlockSpec((None, tq, LANES), lambda i, p, t: (i, t, p)),
        out_shape=jax.ShapeDtypeStruct((b, s, D_ATTN), BF16),
        scratch_shapes=[pltpu.VMEM((2, 2 * LANES, tq), BF16),
                        pltpu.VMEM((2, 2 * LANES, tq), BF16),
                        pltpu.VMEM((2, tq // 2, tq), F32),
                        pltpu.VMEM((2, tq // 2, tq), F32),
                        pltpu.VMEM((2, 1, tq), F32),
                        pltpu.VMEM((2, 1, tq), F32),
                        pltpu.VMEM((2, V_ROWS, tq), F32),
                        pltpu.VMEM((2, 1, tq), F32)],
        compiler_params=_params("arbitrary", "arbitrary", "arbitrary"),
        name="fox_attention",
    )(qx, kx, vt)


def _outproj_kernel(alpha, yr_ref, ya_ref, za_ref, x_ref, gate_ref, na_ref, w_ref,
                    g_ref, b_ref, o_ref):
    n_chunk = x_ref.shape[0] // OUT_CHUNK
    for r in range(n_chunk):
        rows = slice(r * OUT_CHUNK, (r + 1) * OUT_CHUNK)
        ya = ya_ref[rows, :].astype(F32)
        ms = jnp.mean(ya * ya, axis=-1, keepdims=True)
        za = za_ref[rows, :].astype(F32)
        ya = ya * lax.rsqrt(ms + LN_EPS) * na_ref[...] * (za * jax.nn.sigmoid(za))
        y = (jnp.dot(yr_ref[rows, :], w_ref[0:D_LRU, :], preferred_element_type=F32)
             + jnp.dot(ya.astype(BF16), w_ref[D_LRU:, :], preferred_element_type=F32))
        res = alpha * x_ref[rows, :] + gate_ref[...] * y
        mu = jnp.mean(res, axis=-1, keepdims=True)
        rc = res - mu
        var = jnp.mean(rc * rc, axis=-1, keepdims=True)
        o_ref[rows, :] = rc * lax.rsqrt(var + LN_EPS) * g_ref[...] + b_ref[...]


def _outproj(l, alpha, yr, ya, za, x, mod, na, w, g, bb):
    b, s, _ = x.shape
    tm = OUT_TILE
    tok = lambda w_: pl.BlockSpec((None, tm, w_), lambda i, t: (i, t, 0))
    return pl.pallas_call(
        functools.partial(_outproj_kernel, alpha),
        grid=(b, s // tm),
        in_specs=[tok(D_LRU), tok(D_ATTN), tok(D_ATTN), tok(D_MODEL), _mod_spec(l, 2)]
                 + [_layer_spec(a, l) for a in (na, w, g, bb)],
        out_specs=tok(D_MODEL),
        out_shape=jax.ShapeDtypeStruct((b, s, D_MODEL), F32),
        compiler_params=_params("arbitrary", "arbitrary"),
        name="outproj_ln",
    )(yr, ya, za, x, mod, na, w, g, bb)


def _bias_placement():
    pq = np.zeros((LANES, D_ATTN), np.float32)
    pk = np.zeros((LANES, D_ATTN), np.float32)
    one = 3 * N_HEADS
    for hd in range(N_HEADS):
        base = (hd // 2) * LANES + (hd % 2) * 6
        for part in range(3):
            src = part * N_HEADS + hd
            pq[src, base + part] = 1.0
            pq[one, base + 3 + part] = 1.0
            pk[one, base + part] = 1.0
            pk[src, base + 3 + part] = -1.0
    return jnp.asarray(pq, BF16), jnp.asarray(pk, BF16)


def _block_diag(w):
    depth, nb, bw, _ = w.shape
    eye = jnp.eye(nb, dtype=w.dtype)
    return (eye[:, None, :, None] * w[:, :, :, None, :]).reshape(depth, nb * bw, nb * bw)


def kernel(x, c, w_ada, b_ada, w_in, b_fgate, conv_w, conv_b, w_gate_a, b_gate_a,
           w_gate_x, b_gate_x, lru_lambda, norm_lru, norm_attn, w_out, ln_gain, ln_bias):
    depth = w_ada.shape[0]
    batch = x.shape[0]
    alpha = (2.0 * depth) ** 0.25
    n_main = 2 * D_LRU + 4 * D_ATTN
    n_gate = 3 * N_HEADS

    c8 = jnp.pad(c, ((0, SUBLANES - batch), (0, 0)))
    mod = _adaln(c8, w_ada, b_ada).reshape(depth, SUBLANES, 1, 3 * D_MODEL)
    pq, pk = _bias_placement()

    row = lambda a: a.reshape(depth, 1, a.shape[-1])
    w_all = w_in.astype(BF16)
    wf = jnp.zeros((depth, D_MODEL, LANES), F32).at[:, :, :n_gate].set(
        jnp.tile(w_in[:, :, n_main:], (1, 1, 3))).astype(BF16)
    bf = jnp.zeros((depth, 1, LANES), F32).at[:, 0, :n_gate].set(jnp.tile(b_fgate, (1, 3)))
    wg = jnp.concatenate([_block_diag(w_gate_a), _block_diag(w_gate_x)], axis=-1).astype(BF16)
    bg = row(jnp.concatenate([b_gate_a, b_gate_x], axis=-1))
    w_o = w_out.astype(BF16)
    cb, lam, gn, na, lg, lb = (row(a) for a in (conv_b, lru_lambda, norm_lru, norm_attn,
                                                 ln_gain, ln_bias))

    for l in range(depth):
        yr, za, qx, kx, vt = _inproj(l, x, mod, w_all, wf, bf, pq, pk, conv_w, cb, wg, bg,
                                     lam, gn)
        ya = _attention(qx, kx, vt)
        x = _outproj(l, alpha, yr, ya, za, x, mod, na, w_o, lg, lb)
    return x
```

```python
import functools

import numpy as np
import jax
import jax.numpy as jnp
from jax import lax
from jax.experimental import pallas as pl
from jax.experimental.pallas import tpu as pltpu

F32 = jnp.float32
BF16 = jnp.bfloat16

D_MODEL = 1024
D_LRU = 512
D_ATTN = 512
N_HEADS = 8
HEAD_DIM = 64
N_PAIRS = N_HEADS // 2
LRU_BLOCKS = 8
CONV_W = 4
LRU_C = 8.0
LN_EPS = 1e-5
MASK_VALUE = -1e30
TINY = 1e-37
LOG2E = 1.4426950408889634

SUBLANES = 8
LANES = 128
BF16_ROWS = 16
V_ROWS = HEAD_DIM + BF16_ROWS

TOKEN_TILE = 512
OUT_TILE = 1024
OUT_CHUNK = 256
Q_TILE = 1024
ATTN_STRIP = 256
KV_TILE = TOKEN_TILE
VMEM_LIMIT = 56 * 1024 * 1024


def _params(*sem):
    return pltpu.CompilerParams(dimension_semantics=sem, vmem_limit_bytes=VMEM_LIMIT)


def _adaln_kernel(c_ref, w_ref, b_ref, o_ref):
    c = c_ref[...]
    act = c * jax.nn.sigmoid(c)
    o_ref[...] = jnp.dot(act, w_ref[...], precision=lax.Precision.HIGHEST,
                         preferred_element_type=F32) + b_ref[...]


def _adaln(c8, w_ada, b_ada):
    depth, _, n = w_ada.shape
    bn = 1024
    return pl.pallas_call(
        _adaln_kernel,
        grid=(depth, n // bn),
        in_specs=[
            pl.BlockSpec((SUBLANES, D_MODEL), lambda l, j: (0, 0)),
            pl.BlockSpec((None, D_MODEL, bn), lambda l, j: (l, 0, j)),
            pl.BlockSpec((None, 1, bn), lambda l, j: (l, 0, j)),
        ],
        out_specs=pl.BlockSpec((None, SUBLANES, bn), lambda l, j: (l, 0, j)),
        out_shape=jax.ShapeDtypeStruct((depth, SUBLANES, n), F32),
        compiler_params=_params("arbitrary", "arbitrary"),
        name="adaln_mod",
    )(c8, w_ada, b_ada.reshape(depth, 1, n))


def _cumsum_rows(x):
    n = x.shape[0]
    row = lax.broadcasted_iota(jnp.int32, x.shape, 0)
    s = 1
    while s < n:
        x = x + jnp.where(row >= s, pltpu.roll(x, s, 0), 0.0)
        s *= 2
    return x


def _inproj_kernel(x_ref, scale_ref, shift_ref, w_ref, wf_ref, bf_ref,
                   pq_ref, pk_ref, cw_ref, cb_ref, wg_ref, bg_ref, lam_ref, gn_ref,
                   yr_ref, za_ref, qx_ref, kx_ref, vt_ref,
                   dcarry_ref, xbuf, zbuf, ybuf, h7_s, p7_s, c_s, hcarry):
    tm = x_ref.shape[0]
    n_slab = D_LRU // LANES

    @pl.when(pl.program_id(1) == 0)
    def _():
        dcarry_ref[...] = jnp.zeros_like(dcarry_ref)
        xbuf[:, 0:SUBLANES, :] = jnp.zeros((n_slab, SUBLANES, LANES), F32)
        hcarry[...] = jnp.zeros_like(hcarry)

    half_w = 2 * LANES

    def layer_norm(rows):
        x = x_ref[rows, :]
        mu = jnp.mean(x, axis=-1, keepdims=True)
        xc = x - mu
        var = jnp.mean(xc * xc, axis=-1, keepdims=True)
        h = xc * lax.rsqrt(var + LN_EPS) * (1.0 + scale_ref[...]) + shift_ref[...]
        return h.astype(BF16)

    def to_slabs(dst, row0, val, half):
        for s2 in range(2):
            dst[2 * half + s2, row0:row0 + val.shape[0], :] = val[:, s2 * LANES:(s2 + 1) * LANES]

    hm = tm // 2
    hb_parts = []
    for r in range(2):
        hb_r = layer_norm(slice(r * hm, (r + 1) * hm))
        hb_parts.append(hb_r)
        for half in range(2):
            xr = jnp.dot(hb_r, w_ref[:, half * half_w:(half + 1) * half_w],
                         preferred_element_type=F32)
            to_slabs(xbuf, SUBLANES + r * hm, xr, half)
    hb = jnp.concatenate(hb_parts, axis=0)

    def proj(lo):
        return jnp.dot(hb, w_ref[:, lo:lo + half_w], preferred_element_type=F32)

    conv = _lru_conv(xbuf, cw_ref, cb_ref, tm)
    for half in range(2):
        to_slabs(zbuf, 0, proj(D_LRU + half * half_w), half)
    g = jnp.dot(conv.astype(BF16), wg_ref[...], preferred_element_type=F32) + bg_ref[...]
    nl = -lam_ref[...]
    sp8 = LRU_C * (jnp.maximum(nl, 0.0) + jnp.log1p(jnp.exp(-jnp.abs(nl))))

    def piece(n):
        kind, half = divmod(n, 2)
        if kind == 0:
            val = proj(2 * D_LRU + 3 * D_ATTN + half * half_w)
            za_ref[:, half * half_w:(half + 1) * half_w] = val.astype(za_ref.dtype)
        elif kind in (1, 2):
            dst, lo, mul = ((qx_ref, 2 * D_LRU, HEAD_DIM ** -0.5 * LOG2E) if kind == 1
                            else (kx_ref, 2 * D_LRU + D_ATTN, None))
            val = proj(lo + half * half_w)
            if mul is not None:
                val = val * mul
            for s2 in range(2):
                p = 2 * half + s2
                dst[:, 2 * p * LANES:(2 * p + 1) * LANES] = (
                    val[:, s2 * LANES:(s2 + 1) * LANES].astype(BF16))
        else:
            val = proj(2 * D_LRU + 2 * D_ATTN + half * half_w)
            vt = val.T
            ones = jnp.ones((BF16_ROWS, tm), BF16)
            for h4 in range(N_HEADS // 2):
                hd = half * (N_HEADS // 2) + h4
                vt_ref[hd * V_ROWS:hd * V_ROWS + HEAD_DIM, :] = (
                    vt[h4 * HEAD_DIM:(h4 + 1) * HEAD_DIM, :].astype(BF16))
                vt_ref[hd * V_ROWS + HEAD_DIM:(hd + 1) * V_ROWS, :] = ones
        return val[0:SUBLANES, :]

    def zero_after(tok):
        bits = pltpu.bitcast(tok, jnp.uint32)
        bits = lax.shift_right_logical(lax.shift_right_logical(bits, jnp.uint32(16)),
                                       jnp.uint32(16))
        row = pltpu.bitcast(bits, F32)[0:1, :]
        return jnp.concatenate([row, row], axis=1)

    ng = tm // SUBLANES
    hs, ps = [], []
    tok = None
    for j in range(SUBLANES):
        rows = slice(j * ng, (j + 1) * ng)
        sp_j = sp8 if tok is None else sp8 + zero_after(tok)
        aj, uj = _lru_gates(g[rows, :], conv[rows, :], sp_j)
        if j == 0:
            hl, pr = uj, aj
        else:
            hl = aj * hl + uj
            pr = aj * pr
        hs.append(hl)
        ps.append(pr)
        tok = piece(j)

    fl = jnp.dot(hb, wf_ref[...], preferred_element_type=F32) + bf_ref[...]
    logf = jnp.minimum(fl, 0.0) - jnp.log1p(jnp.exp(-jnp.abs(fl)))
    d = _cumsum_rows(logf) + dcarry_ref[0:1, :]
    dcarry_ref[...] = jnp.broadcast_to(d[tm - 1:tm, :], dcarry_ref.shape)
    d2 = d * LOG2E
    hi = d2.astype(BF16).astype(F32)
    r1 = d2 - hi
    mid = r1.astype(BF16).astype(F32)
    lo = (r1 - mid).astype(BF16).astype(F32)
    lane = lax.broadcasted_iota(jnp.int32, d2.shape, 1)
    dsplit = jnp.where(lane < N_HEADS, hi,
             jnp.where(lane < 2 * N_HEADS, mid,
             jnp.where(lane < 3 * N_HEADS, lo,
             jnp.where(lane == 3 * N_HEADS, 1.0, 0.0)))).astype(BF16)
    qbias = jnp.dot(dsplit, pq_ref[...], preferred_element_type=F32)
    kbias = jnp.dot(dsplit, pk_ref[...], preferred_element_type=F32)
    for p in range(N_PAIRS):
        src = slice(p * LANES, (p + 1) * LANES)
        qx_ref[:, (2 * p + 1) * LANES:(2 * p + 2) * LANES] = qbias[:, src].astype(BF16)
        kx_ref[:, (2 * p + 1) * LANES:(2 * p + 2) * LANES] = kbias[:, src].astype(BF16)

    _lru_scan_norm(hs, ps, zbuf, ybuf, gn_ref, h7_s, p7_s, c_s, hcarry, tm)
    for sl in range(n_slab):
        yr_ref[:, sl * LANES:(sl + 1) * LANES] = ybuf[sl].astype(yr_ref.dtype)


def _layer_spec(a, l):
    return pl.BlockSpec((None,) + a.shape[1:], lambda i, t: (l,) + (0,) * (a.ndim - 1))


def _mod_spec(l, part):
    return pl.BlockSpec((None, None, 1, D_MODEL), lambda i, t: (l, i, 0, part))


def _inproj(l, x, mod, w_all, wf, bf, pq, pk, cw, cb, wg, bg, lam, gn):
    b, s, _ = x.shape
    tm = TOKEN_TILE
    nt = s // tm
    ng = tm // SUBLANES
    n_slab = D_LRU // LANES
    tok = lambda w: pl.BlockSpec((None, tm, w), lambda i, t: (i, t, 0))
    full = lambda a: pl.BlockSpec(a.shape, lambda i, t: (0,) * a.ndim)
    layered = (w_all, wf, bf)
    return pl.pallas_call(
        _inproj_kernel,
        grid=(b, nt),
        in_specs=([tok(D_MODEL), _mod_spec(l, 1), _mod_spec(l, 0)]
                  + [_layer_spec(a, l) for a in layered] + [full(pq), full(pk)]
                  + [_layer_spec(a, l) for a in (cw, cb, wg, bg, lam, gn)]),
        out_specs=[tok(D_LRU), tok(D_ATTN), tok(2 * D_ATTN), tok(2 * D_ATTN),
                   pl.BlockSpec((None, None, N_HEADS * V_ROWS, tm), lambda i, t: (i, t, 0, 0))],
        out_shape=[jax.ShapeDtypeStruct((b, s, D_LRU), BF16),
                   jax.ShapeDtypeStruct((b, s, D_ATTN), BF16),
                   jax.ShapeDtypeStruct((b, s, 2 * D_ATTN), BF16),
                   jax.ShapeDtypeStruct((b, s, 2 * D_ATTN), BF16),
                   jax.ShapeDtypeStruct((b, nt, N_HEADS * V_ROWS, tm), BF16)],
        scratch_shapes=[pltpu.VMEM((SUBLANES, LANES), F32),
                        pltpu.VMEM((n_slab, tm + SUBLANES, LANES), F32),
                        pltpu.VMEM((n_slab, tm, LANES), F32),
                        pltpu.VMEM((n_slab, tm, LANES), F32),
                        pltpu.VMEM((ng, D_LRU), F32),
                        pltpu.VMEM((ng, D_LRU), F32),
                        pltpu.VMEM((ng, D_LRU), F32),
                        pltpu.VMEM((SUBLANES, D_LRU), F32)],
        compiler_params=_params("arbitrary", "arbitrary"),
        name="ln_inproj_rglru",
    )(x, mod, mod, w_all, wf, bf, pq, pk, cw, cb, wg, bg, lam, gn)


def _lru_conv(xbuf, cw_ref, cb_ref, tc):
    ng = tc // SUBLANES
    n_slab = D_LRU // LANES
    lanes = lambda sl: slice(sl * LANES, (sl + 1) * LANES)
    grp = lambda first: pl.ds(first, ng, stride=SUBLANES)
    blocks = []
    for j in range(SUBLANES):
        cols = []
        for sl in range(n_slab):
            acc = cb_ref[:, lanes(sl)]
            for kk in range(CONV_W):
                first = SUBLANES + j - (CONV_W - 1) + kk
                acc = acc + cw_ref[kk:kk + 1, lanes(sl)] * xbuf[sl, grp(first), :]
            cols.append(acc)
        blocks.append(jnp.concatenate(cols, axis=1))
    for sl in range(n_slab):
        xbuf[sl, 0:SUBLANES, :] = xbuf[sl, tc:tc + SUBLANES, :]
    return jnp.concatenate(blocks, axis=0)


def _lru_gates(g, xc, sp8):
    r = jax.nn.sigmoid(g[:, :D_LRU])
    i = jax.nn.sigmoid(g[:, D_LRU:])
    w = r * sp8
    a = jnp.exp(-w)
    m2 = jnp.tanh(w) * (1.0 + a * a)
    mult = m2 * lax.rsqrt(jnp.maximum(m2, TINY))
    return a, mult * (i * xc)


def _lru_scan_norm(hs, ps, zbuf, ybuf, g_ref, h7_s, p7_s, c_s, hcarry, tc):
    ng = tc // SUBLANES
    n_slab = D_LRU // LANES
    lanes = lambda sl: slice(sl * LANES, (sl + 1) * LANES)
    grp = lambda first: pl.ds(first, ng, stride=SUBLANES)
    h7_s[...] = hs[-1]
    p7_s[...] = ps[-1]

    c = hcarry[0:1, :]
    for gi in range(ng):
        c_s[gi:gi + 1, :] = c
        c = p7_s[gi:gi + 1, :] * c + h7_s[gi:gi + 1, :]
    hcarry[0:1, :] = c
    cin = c_s[...]

    for j in range(SUBLANES):
        hj = hs[j] + ps[j] * cin
        ms = jnp.mean(hj * hj, axis=-1, keepdims=True)
        z = jnp.concatenate([zbuf[sl, grp(j), :] for sl in range(n_slab)], axis=1)
        y = hj * lax.rsqrt(ms + LN_EPS) * g_ref[...] * (z * jax.nn.sigmoid(z))
        for sl in range(n_slab):
            ybuf[sl, grp(j), :] = y[:, lanes(sl)]


def _attn_kernel(q_ref, k_ref, v_ref, o_ref, q_s, q_n, s_a, s_b, mx_a, mx_b, acc_ref, m_ref):
    tq = o_ref.shape[0]
    tk = tq // 2
    qi = pl.program_id(2)
    nq = pl.num_programs(2)

    def load_q(t, dst):
        qf = q_ref[pl.ds(pl.multiple_of(t * tq, tq), tq), :].astype(F32)
        lane = lax.broadcasted_iota(jnp.int32, qf.shape, 1)
        nb = 6
        in_a = (lane < HEAD_DIM) | ((lane >= 2 * HEAD_DIM) & (lane < 2 * HEAD_DIM + nb))
        in_b = (((lane >= HEAD_DIM) & (lane < 2 * HEAD_DIM))
                | ((lane >= 2 * HEAD_DIM + nb) & (lane < 2 * HEAD_DIM + 2 * nb)))
        dst[0] = jnp.where(in_a, qf, 0.0).T.astype(BF16)
        dst[1] = jnp.where(in_b, qf, 0.0).T.astype(BF16)

    bufs = ((s_a, mx_a), (s_b, mx_b))

    def skipped(st, half, diag):
        return diag and st * ATTN_STRIP < half * tk

    def n_keys(st, half, diag):
        return min(tk, (st + 1) * ATTN_STRIP - half * tk) if diag else tk

    def score_strip(st, hd, kv, half, diag, q_src=q_s):
        s_ref, mx_ref = bufs[half]
        cols = slice(st * ATTN_STRIP, (st + 1) * ATTN_STRIP)
        nk = n_keys(st, half, diag)
        start = pl.multiple_of(kv * tq + half * tk, tk)
        k = k_ref[pl.ds(start, nk), :]
        s = jnp.dot(k, q_src[hd, :, cols], preferred_element_type=F32)
        if diag and half * tk + nk - 1 > st * ATTN_STRIP:
            kpos = half * tk + lax.broadcasted_iota(jnp.int32, s.shape, 0)
            qpos = st * ATTN_STRIP + lax.broadcasted_iota(jnp.int32, s.shape, 1)
            s = jnp.where(kpos <= qpos, s, MASK_VALUE)
        s_ref[hd, 0:nk, cols] = s
        mx_ref[hd, :, cols] = jnp.max(s, axis=0, keepdims=True)

    def consume_strip(st, hd, kv, half, diag=False):
        s_ref, mx_ref = bufs[half]
        cols = slice(st * ATTN_STRIP, (st + 1) * ATTN_STRIP)
        nk = n_keys(st, half, diag)
        m_old = m_ref[hd, :, cols]
        m_new = jnp.maximum(m_old, mx_ref[hd, :, cols])
        alpha = jnp.exp2(m_old - m_new)
        p = jnp.exp2(s_ref[hd, 0:nk, cols] - m_new).astype(BF16)
        rows = slice(hd * V_ROWS, (hd + 1) * V_ROWS)
        if v_ref.shape[-1] == tk:
            vt = v_ref[2 * kv + half, rows, 0:nk]
        else:
            vt = v_ref[kv, rows, half * tk:half * tk + nk]
        acc_ref[hd, :, cols] = (alpha * acc_ref[hd, :, cols]
                                + jnp.dot(vt, p, preferred_element_type=F32))
        m_ref[hd, :, cols] = m_new

    def step(score_args, consume_args):
        for st in range(tq // ATTN_STRIP):
            for hd in range(2):
                if score_args is not None and not skipped(st, *score_args[1:3]):
                    score_strip(st, hd, *score_args)
                if consume_args is not None and not skipped(st, *consume_args[1:3]):
                    consume_strip(st, hd, *consume_args)

    m_ref[...] = jnp.full(m_ref.shape, MASK_VALUE, F32)
    acc_ref[...] = jnp.zeros_like(acc_ref)

    def full_step(kv):
        step((kv, 1, False), (kv, 0, False))
        step((kv + 1, 0, False), (kv, 1, False))

    def finish(with_prev, prefetch_next):
        if with_prev:
            step((qi - 1, 1, False), (qi - 1, 0, False))
            step((qi, 0, True), (qi - 1, 1, False))
        step((qi, 1, True), (qi, 0, True))
        if prefetch_next:
            load_q(qi + 1, q_n)
            step((0, 0, False, q_n), (qi, 1, True))
        else:
            step(None, (qi, 1, True))

    @pl.when(qi == 0)
    def _():
        load_q(0, q_s)
        step((0, 0, True), None)
        finish(False, True)

    @pl.when(qi > 0)
    def _():
        q_s[...] = q_n[...]

    n_full = jnp.maximum(qi - 1, 0)

    def pair_step(i, carry):
        full_step(2 * i)
        full_step(2 * i + 1)
        return carry

    lax.fori_loop(0, n_full // 2, pair_step, 0)

    @pl.when(n_full % 2 == 1)
    def _():
        full_step(n_full - 1)

    @pl.when((qi > 0) & (qi + 1 < nq))
    def _():
        finish(True, True)

    @pl.when((qi > 0) & (qi + 1 == nq))
    def _():
        finish(True, False)

    outs = []
    for hd in range(2):
        acc = acc_ref[hd]
        outs.append(acc[:HEAD_DIM, :] / acc[HEAD_DIM:HEAD_DIM + 1, :])
    o_ref[...] = jnp.concatenate(outs, axis=0).T.astype(o_ref.dtype)


def _attention(qx, kx, vt):
    b, s, _ = qx.shape
    nkv, _, tk = vt.shape[1:]
    tq = Q_TILE
    assert tq // 2 in (tk, tk // 2)
    assert s // tq >= 2
    return pl.pallas_call(
        _attn_kernel,
        grid=(b, N_PAIRS, s // tq),
        in_specs=[
            pl.BlockSpec((None, s, 2 * LANES), lambda i, p, t: (i, 0, p)),
            pl.BlockSpec((None, s, 2 * LANES), lambda i, p, t: (i, 0, p)),
            pl.BlockSpec((None, nkv, 2 * V_ROWS, tk), lambda i, p, t: (i, 0, p, 0)),
        ],
        out_specs=pl.BlockSpec((None, tq, LANES), lambda i, p, t: (i, t, p)),
        out_shape=jax.ShapeDtypeStruct((b, s, D_ATTN), BF16),
        scratch_shapes=[pltpu.VMEM((2, 2 * LANES, tq), BF16),
                        pltpu.VMEM((2, 2 * LANES, tq), BF16),
                        pltpu.VMEM((2, tq // 2, tq), F32),
                        pltpu.VMEM((2, tq // 2, tq), F32),
                        pltpu.VMEM((2, 1, tq), F32),
                        pltpu.VMEM((2, 1, tq), F32),
                        pltpu.VMEM((2, V_ROWS, tq), F32),
                        pltpu.VMEM((2, 1, tq), F32)],
        compiler_params=_params("arbitrary", "arbitrary", "arbitrary"),
        name="fox_attention",
    )(qx, kx, vt)


def _outproj_kernel(alpha, yr_ref, ya_ref, za_ref, x_ref, gate_ref, na_ref, w_ref,
                    g_ref, b_ref, o_ref):
    n_chunk = x_ref.shape[0] // OUT_CHUNK
    for r in range(n_chunk):
        rows = slice(r * OUT_CHUNK, (r + 1) * OUT_CHUNK)
        ya = ya_ref[rows, :].astype(F32)
        ms = jnp.mean(ya * ya, axis=-1, keepdims=True)
        za = za_ref[rows, :].astype(F32)
        ya = ya * lax.rsqrt(ms + LN_EPS) * na_ref[...] * (za * jax.nn.sigmoid(za))
        y = (jnp.dot(yr_ref[rows, :], w_ref[0:D_LRU, :], preferred_element_type=F32)
             + jnp.dot(ya.astype(BF16), w_ref[D_LRU:, :], preferred_element_type=F32))
        res = alpha * x_ref[rows, :] + gate_ref[...] * y
        mu = jnp.mean(res, axis=-1, keepdims=True)
        rc = res - mu
        var = jnp.mean(rc * rc, axis=-1, keepdims=True)
        o_ref[rows, :] = rc * lax.rsqrt(var + LN_EPS) * g_ref[...] + b_ref[...]


def _outproj(l, alpha, yr, ya, za, x, mod, na, w, g, bb):
    b, s, _ = x.shape
    tm = OUT_TILE
    tok = lambda w_: pl.BlockSpec((None, tm, w_), lambda i, t: (i, t, 0))
    return pl.pallas_call(
        functools.partial(_outproj_kernel, alpha),
        grid=(b, s // tm),
        in_specs=[tok(D_LRU), tok(D_ATTN), tok(D_ATTN), tok(D_MODEL), _mod_spec(l, 2)]
                 + [_layer_spec(a, l) for a in (na, w, g, bb)],
        out_specs=tok(D_MODEL),
        out_shape=jax.ShapeDtypeStruct((b, s, D_MODEL), F32),
        compiler_params=_params("arbitrary", "arbitrary"),
        name="outproj_ln",
    )(yr, ya, za, x, mod, na, w, g, bb)


def _bias_placement():
    pq = np.zeros((LANES, D_ATTN), np.float32)
    pk = np.zeros((LANES, D_ATTN), np.float32)
    one = 3 * N_HEADS
    for hd in range(N_HEADS):
        base = (hd // 2) * LANES + (hd % 2) * 6
        for part in range(3):
            src = part * N_HEADS + hd
            pq[src, base + part] = 1.0
            pq[one, base + 3 + part] = 1.0
            pk[one, base + part] = 1.0
            pk[src, base + 3 + part] = -1.0
    return jnp.asarray(pq, BF16), jnp.asarray(pk, BF16)


def _block_diag(w):
    depth, nb, bw, _ = w.shape
    eye = jnp.eye(nb, dtype=w.dtype)
    return (eye[:, None, :, None] * w[:, :, :, None, :]).reshape(depth, nb * bw, nb * bw)


def kernel(x, c, w_ada, b_ada, w_in, b_fgate, conv_w, conv_b, w_gate_a, b_gate_a,
           w_gate_x, b_gate_x, lru_lambda, norm_lru, norm_attn, w_out, ln_gain, ln_bias):
    depth = w_ada.shape[0]
    batch = x.shape[0]
    alpha = (2.0 * depth) ** 0.25
    n_main = 2 * D_LRU + 4 * D_ATTN
    n_gate = 3 * N_HEADS

    c8 = jnp.pad(c, ((0, SUBLANES - batch), (0, 0)))
    mod = _adaln(c8, w_ada, b_ada).reshape(depth, SUBLANES, 1, 3 * D_MODEL)
    pq, pk = _bias_placement()

    row = lambda a: a.reshape(depth, 1, a.shape[-1])
    w_all = w_in.astype(BF16)
    wf = jnp.zeros((depth, D_MODEL, LANES), F32).at[:, :, :n_gate].set(
        jnp.tile(w_in[:, :, n_main:], (1, 1, 3))).astype(BF16)
    bf = jnp.zeros((depth, 1, LANES), F32).at[:, 0, :n_gate].set(jnp.tile(b_fgate, (1, 3)))
    wg = jnp.concatenate([_block_diag(w_gate_a), _block_diag(w_gate_x)], axis=-1).astype(BF16)
    bg = row(jnp.concatenate([b_gate_a, b_gate_x], axis=-1))
    w_o = w_out.astype(BF16)
    cb, lam, gn, na, lg, lb = (row(a) for a in (conv_b, lru_lambda, norm_lru, norm_attn,
                                                 ln_gain, ln_bias))

    for l in range(depth):
        yr, za, qx, kx, vt = _inproj(l, x, mod, w_all, wf, bf, pq, pk, conv_w, cb, wg, bg,
                                     lam, gn)
        ya = _attention(qx, kx, vt)
        x = _outproj(l, alpha, yr, ya, za, x, mod, na, w_o, lg, lb)
    return x
```
